```python
import jax, jax.numpy as jnp
from jax import lax
import numpy as np

D_MODEL = 1024
BATCH = 16
SEQ = 4096
DEPTH = 4

GRID_W = 64
CTX_LEN = 256
HEAD_DIM = 64
W_SC = D_MODEL // 4
W_NA = D_MODEL // 2
NA_HEADS = W_NA // HEAD_DIM
W_CF = D_MODEL // 4
D_MIX = W_SC + W_NA + W_CF
SC_K = 3
CF_K = 31
NA_ROWS = 8
NA_COLS = 16
NORM_EPS = 1e-6
LN_EPS = 1e-5
PROJ_SIZES = [W_SC] * 4 + [W_NA] * 4 + [W_CF] * 3
PROJ_SPLITS = [int(s) for s in np.cumsum(PROJ_SIZES)[:-1]]
D_PROJ = int(sum(PROJ_SIZES))

kernel_name = "hybrid_conv_natten_conformer_dit"


def rmsnorm(x, g):
    xf = x.astype(jnp.float32)
    y = xf * lax.rsqrt(jnp.mean(xf * xf, axis=-1, keepdims=True) + NORM_EPS)
    return (y * g.astype(jnp.float32)).astype(x.dtype)


def layernorm(x, g, b):
    xf = x.astype(jnp.float32)
    mu = jnp.mean(xf, axis=-1, keepdims=True)
    var = jnp.mean(jnp.square(xf - mu), axis=-1, keepdims=True)
    y = (xf - mu) * lax.rsqrt(var + LN_EPS)
    return (y * g.astype(jnp.float32) + b.astype(jnp.float32)).astype(x.dtype)


def dwconv(x, w):
    k = w.shape[0]
    return lax.conv_general_dilated(
        x, w[:, None, :].astype(x.dtype), window_strides=(1,),
        padding=[(k // 2, k // 2)], dimension_numbers=('NWC', 'WIO', 'NWC'),
        feature_group_count=x.shape[-1])


def short_conv_branch(z_h, z_b, z_c, z_g, w_conv):
    return z_b * dwconv(z_c * z_h, w_conv) * jax.nn.silu(z_g)


def conformer_branch(z_p, z_q, z_g, w_conv, b_conv, ln_g, ln_b):
    u = z_p * jax.nn.sigmoid(z_q)
    u = dwconv(u, w_conv) + b_conv.astype(u.dtype)
    u = layernorm(u, ln_g, ln_b)
    return jax.nn.silu(u) * jax.nn.silu(z_g)


def heads(t):
    b, l, _ = t.shape
    return t.reshape(b, l, NA_HEADS, HEAD_DIM)


def neighbourhood_attention(q, k, v, kc, vc, rpb):
    b, l, h, dh = q.shape
    rows = l // GRID_W
    kr = min(NA_ROWS, rows)
    scale = HEAD_DIM ** -0.5
    qg = q.reshape(b, rows, GRID_W, h, dh)
    kg = k.reshape(b, rows, GRID_W, h, dh)
    vg = v.reshape(b, rows, GRID_W, h, dh)
    cols = jnp.arange(GRID_W)
    col_start = jnp.clip(cols - NA_COLS // 2, 0, GRID_W - NA_COLS)
    col_idx = col_start[:, None] + jnp.arange(NA_COLS)[None, :]
    col_bias_idx = col_idx - cols[:, None] + (NA_COLS - 1)
    rpb_cols = rpb[:, :, col_bias_idx]

    def row_block(r):
        rs = jnp.clip(r - kr // 2, 0, rows - kr)
        kb = lax.dynamic_slice_in_dim(kg, rs, kr, axis=1)
        vb = lax.dynamic_slice_in_dim(vg, rs, kr, axis=1)
        qr = lax.dynamic_index_in_dim(qg, r, axis=1, keepdims=False)
        kw = kb[:, :, col_idx]
        vw = vb[:, :, col_idx]
        row_bias_idx = rs + jnp.arange(kr) - r + (NA_ROWS - 1)
        bias = jnp.take(rpb_cols, row_bias_idx, axis=1)
        bias = bias.transpose(0, 2, 1, 3).astype(jnp.float32)
        s_win = jnp.einsum('bqhd,brqchd->bhqrc', qr, kw).astype(jnp.float32) * scale + bias[None]
        s_ctx = jnp.einsum('bqhd,bkhd->bhqk', qr, kc).astype(jnp.float32) * scale
        s = jnp.concatenate([s_win.reshape(b, h, GRID_W, kr * NA_COLS), s_ctx], axis=-1)
        p = jax.nn.softmax(s, axis=-1).astype(v.dtype)
        pw = p[..., :kr * NA_COLS].reshape(b, h, GRID_W, kr, NA_COLS)
        pc = p[..., kr * NA_COLS:]
        return (jnp.einsum('bhqrc,brqchd->bqhd', pw, vw)
                + jnp.einsum('bhqk,bkhd->bqhd', pc, vc))

    out = lax.map(row_block, jnp.arange(rows))
    return out.transpose(1, 0, 2, 3, 4).reshape(b, l, h * dh)


def context_attention(q, k, v):
    b, n, h, dh = q.shape
    s = jnp.einsum('bqhd,bkhd->bhqk', q, k).astype(jnp.float32) * (HEAD_DIM ** -0.5)
    p = jax.nn.softmax(s, axis=-1).astype(v.dtype)
    return jnp.einsum('bhqk,bkhd->bqhd', p, v).reshape(b, n, h * dh)


def setup_inputs(seed: int = 0) -> dict:
    key = jax.random.key(seed)
    ks = jax.random.split(key, 16)
    f32 = jnp.float32
    nrm = lambda k, shape, s: (jax.random.normal(k, shape, f32) * s).astype(f32)
    return {
        "x": nrm(ks[0], (BATCH, SEQ, D_MODEL), 1.0),
        "c": nrm(ks[1], (BATCH, D_MODEL), 1.0),
        "ctx": nrm(ks[2], (BATCH, CTX_LEN, D_MODEL), 1.0),
        "c_ctx": nrm(ks[3], (D_MODEL,), 1.0),
        "norm_g": 1.0 + nrm(ks[4], (DEPTH, D_MODEL), 0.02),
        "w_ada": nrm(ks[5], (DEPTH, D_MODEL, 3 * D_MODEL), 0.5 * D_MODEL ** -0.5),
        "b_ada": nrm(ks[6], (DEPTH, 3 * D_MODEL), 0.02),
        "w_in": nrm(ks[7], (DEPTH, D_MODEL, D_PROJ), D_MODEL ** -0.5),
        "conv_sc": nrm(ks[8], (DEPTH, SC_K, W_SC), SC_K ** -0.5),
        "rpb": nrm(ks[9], (DEPTH, NA_HEADS, 2 * NA_ROWS - 1, 2 * NA_COLS - 1), 0.1),
        "conv_cf": nrm(ks[10], (DEPTH, CF_K, W_CF), CF_K ** -0.5),
        "conv_cf_b": nrm(ks[11], (DEPTH, W_CF), 0.02),
        "ln_cf_g": 1.0 + nrm(ks[12], (DEPTH, W_CF), 0.02),
        "ln_cf_b": nrm(ks[13], (DEPTH, W_CF), 0.02),
        "w_out": nrm(ks[14], (DEPTH, D_MIX, D_MODEL), D_MIX ** -0.5),
        "final_g": 1.0 + nrm(ks[15], (D_MODEL,), 0.02),
    }


def reference(x, c, ctx, c_ctx, norm_g, w_ada, b_ada, w_in, conv_sc, rpb,
              conv_cf, conv_cf_b, ln_cf_g, ln_cf_b, w_out, final_g):
    for l in range(DEPTH):
        last = l == DEPTH - 1
        mod = jax.nn.silu(c) @ w_ada[l] + b_ada[l]
        shift, scale, gate = jnp.split(mod, 3, axis=-1)
        mod_c = jax.nn.silu(c_ctx) @ w_ada[l] + b_ada[l]
        shift_c, scale_c, gate_c = jnp.split(mod_c, 3, axis=-1)

        hl = rmsnorm(x, norm_g[l]) * (1.0 + scale[:, None, :]) + shift[:, None, :]
        hc = rmsnorm(ctx, norm_g[l]) * (1.0 + scale_c) + shift_c

        zl = jnp.split(hl @ w_in[l], PROJ_SPLITS, axis=-1)
        zc = jnp.split(hc @ w_in[l], PROJ_SPLITS, axis=-1)
        kc, vc = heads(zc[5]), heads(zc[6])

        ya = short_conv_branch(zl[0], zl[1], zl[2], zl[3], conv_sc[l])
        yb = neighbourhood_attention(heads(zl[4]), heads(zl[5]), heads(zl[6]), kc, vc, rpb[l])
        yb = yb * jax.nn.silu(zl[7])
        yc = conformer_branch(zl[8], zl[9], zl[10], conv_cf[l], conv_cf_b[l], ln_cf_g[l], ln_cf_b[l])
        y = jnp.concatenate([ya, yb, yc], axis=-1) @ w_out[l]

        if not last:
            ca = short_conv_branch(zc[0], zc[1], zc[2], zc[3], conv_sc[l])
            cb = context_attention(heads(zc[4]), kc, vc) * jax.nn.silu(zc[7])
            cc = conformer_branch(zc[8], zc[9], zc[10], conv_cf[l], conv_cf_b[l], ln_cf_g[l], ln_cf_b[l])
            yctx = jnp.concatenate([ca, cb, cc], axis=-1) @ w_out[l]
            ctx = ctx + gate_c * yctx

        x = x + gate[:, None, :] * y
    return rmsnorm(x, final_g)
```

```python
import functools

import numpy as np
import jax
import jax.numpy as jnp
from jax import lax
from jax.experimental import pallas as pl
from jax.experimental.pallas import tpu as pltpu

F32 = jnp.float32
BF16 = jnp.bfloat16

GRID_W = 64
HEAD_DIM = 64
NA_ROWS = 8
NA_COLS = 16
SC_K = 3
CF_K = 31
NORM_EPS = 1e-6
LN_EPS = 1e-5
MASK_BIAS = -1e30

LANES = 128
HALO_U = 16
VMEM_LIMIT = 56 * 1024 * 1024


def _silu(v):
    return v * (1.0 / (1.0 + jnp.exp(-v)))


def _sigmoid(v):
    return 1.0 / (1.0 + jnp.exp(-v))


def _mod_kernel(c_ref, w_ref, b_ref, o_ref):
    a = _silu(c_ref[...]).astype(BF16)
    o_ref[0] = jnp.dot(a, w_ref[0].astype(BF16), preferred_element_type=F32) + b_ref[0]


def _modulation(cc, w_ada, b_ada):
    depth, d, d3 = w_ada.shape
    rows = cc.shape[0]
    tn = 1024
    return pl.pallas_call(
        _mod_kernel,
        grid=(depth, d3 // tn),
        in_specs=[
            pl.BlockSpec((rows, d), lambda l, n: (0, 0)),
            pl.BlockSpec((1, d, tn), lambda l, n: (l, 0, n)),
            pl.BlockSpec((1, 1, tn), lambda l, n: (l, 0, n)),
        ],
        out_specs=pl.BlockSpec((1, rows, tn), lambda l, n: (l, 0, n)),
        out_shape=jax.ShapeDtypeStruct((depth, rows, d3), F32),
        compiler_params=pltpu.CompilerParams(vmem_limit_bytes=VMEM_LIMIT),
        name="adaln_mod",
    )(cc, w_ada, b_ada.reshape(depth, 1, d3))


def _in_proj_kernel(x_ref, g_ref, scale_ref, shift_ref, w_ref, zt_ref, zu_ref, zkv_ref, *, d):
    w_sc = d // 4
    w_na = d // 2
    x = x_ref[0]
    ms = jnp.mean(x * x, axis=-1, keepdims=True)
    mul = g_ref[...] * (1.0 + scale_ref[0])
    h = (x * lax.rsqrt(ms + NORM_EPS) * mul + shift_ref[0]).astype(BF16)

    def proj(start, width):
        return jnp.dot(h, w_ref[:, start:start + width], preferred_element_type=F32)

    o_na = 4 * w_sc
    o_cf = o_na + 4 * w_na
    zu_ref[0, :, 0:w_sc] = (proj(2 * w_sc, w_sc) * proj(0, w_sc)).astype(BF16)
    zt_ref[0, :, 0:w_sc] = (proj(w_sc, w_sc) * _silu(proj(3 * w_sc, w_sc))).astype(BF16)
    zu_ref[0, :, w_sc:2 * w_sc] = (proj(o_cf, w_sc) * _sigmoid(proj(o_cf + w_sc, w_sc))).astype(BF16)
    zt_ref[0, :, w_sc:2 * w_sc] = _silu(proj(o_cf + 2 * w_sc, w_sc)).astype(BF16)
    zt_ref[0, :, 2 * w_sc:2 * w_sc + w_na] = (proj(o_na, w_na) * (HEAD_DIM ** -0.5)).astype(BF16)
    zkv_ref[0, :, 0:w_na] = proj(o_na + w_na, w_na).astype(BF16)
    zkv_ref[0, :, w_na:2 * w_na] = proj(o_na + 2 * w_na, w_na).astype(BF16)
    zt_ref[0, :, 2 * w_sc + w_na:2 * w_sc + 2 * w_na] = _silu(proj(o_na + 3 * w_na, w_na)).astype(BF16)


def _in_proj(x, g, scale, shift, w, *, tm):
    b, n, d = x.shape
    d_proj = w.shape[1]
    sb = 0 if scale.shape[0] == 1 else 1
    vec = pl.BlockSpec((1, 1, d), lambda bi, i: (bi * sb, 0, 0))
    out = lambda width: pl.BlockSpec((1, tm, width), lambda bi, i: (bi, i, 0))
    return pl.pallas_call(
        functools.partial(_in_proj_kernel, d=d),
        grid=(b, n // tm),
        in_specs=[
            pl.BlockSpec((1, tm, d), lambda bi, i: (bi, i, 0)),
            pl.BlockSpec((1, d), lambda bi, i: (0, 0)),
            vec, vec,
            pl.BlockSpec((d, d_proj), lambda bi, i: (0, 0)),
        ],
        out_specs=[out(d + d // 2), out(d // 2), out(d)],
        out_shape=[
            jax.ShapeDtypeStruct((b, n, d + d // 2), BF16),
            jax.ShapeDtypeStruct((b, n, d // 2), BF16),
            jax.ShapeDtypeStruct((b, n, d), BF16),
        ],
        compiler_params=pltpu.CompilerParams(
            dimension_semantics=("parallel", "parallel"), vmem_limit_bytes=VMEM_LIMIT),
        name="in_proj",
    )(x, g, scale, shift, w)


def _fill_conv_buffer(ubuf, zu_ref, prev_ref, next_ref, tm, first, last):
    width = ubuf.shape[1]
    zeros = jnp.zeros((HALO_U, width), F32)
    if prev_ref is None:
        ubuf[0:HALO_U, :] = zeros
        ubuf[HALO_U + tm:2 * HALO_U + tm, :] = zeros
    else:
        ubuf[0:HALO_U, :] = jnp.where(first, zeros, prev_ref[0].astype(F32))
        ubuf[HALO_U + tm:2 * HALO_U + tm, :] = jnp.where(last, zeros, next_ref[0].astype(F32))
    ubuf[HALO_U:HALO_U + tm, :] = zu_ref[0].astype(F32)


def _dwconv_chunk(ubuf, w_ref, base, rows, lane0, width, ksize):
    acc = None
    for j in range(ksize):
        start = HALO_U + base - ksize // 2 + j
        term = ubuf[start:start + rows, lane0:lane0 + width] * w_ref[j:j + 1, :]
        acc = term if acc is None else acc + term
    return acc


def _conv_branches(ubuf, zt_ref, ybuf, csc_ref, ccf_ref, cfb_ref, lng_ref, lnb_ref, tm, w_sc):
    chunk = 64
    for c0 in range(0, tm, chunk):
        ya = _dwconv_chunk(ubuf, csc_ref, c0, chunk, 0, w_sc, SC_K)
        ya = ya * zt_ref[0, c0:c0 + chunk, 0:w_sc].astype(F32)
        ybuf[c0:c0 + chunk, 0:w_sc] = ya.astype(BF16)
        u = _dwconv_chunk(ubuf, ccf_ref, c0, chunk, w_sc, w_sc, CF_K) + cfb_ref[...]
        mu = jnp.mean(u, axis=-1, keepdims=True)
        uc = u - mu
        var = jnp.mean(uc * uc, axis=-1, keepdims=True)
        un = uc * lax.rsqrt(var + LN_EPS) * lng_ref[...] + lnb_ref[...]
        yc = _silu(un) * zt_ref[0, c0:c0 + chunk, w_sc:2 * w_sc].astype(F32)
        ybuf[c0:c0 + chunk, 3 * w_sc:4 * w_sc] = yc.astype(BF16)


def _split_heads(q2):
    lane = lax.broadcasted_iota(jnp.int32, q2.shape, 1)
    zero = jnp.zeros_like(q2)
    return jnp.concatenate([jnp.where(lane < HEAD_DIM, q2, zero),
                            jnp.where(lane >= HEAD_DIM, q2, zero)], axis=0)


def _merge_heads(o2, rows):
    lane = lax.broadcasted_iota(jnp.int32, (rows, LANES), 1)
    return jnp.where(lane < HEAD_DIM, o2[0:rows], o2[rows:2 * rows])


_NT = (((1,), (1,)), ((), ()))


def _latent_mixer_kernel(x_ref, gate_ref, zt_ref, zu_ref, zup_ref, zun_ref, zkv_ref, zkvp_ref, zkvn_ref,
                         kvc_ref, bias_ref, csc_ref, ccf_ref, cfb_ref, lng_ref, lnb_ref, wout_ref, fg_ref,
                         o_ref, ubuf, kvbuf, ybuf, *, tm, d, rows_total, final):
    w_sc = d // 4
    w_na = d // 2
    halo_kv = zkvp_ref.shape[1]
    halo_rows = halo_kv // GRID_W
    tile_rows = tm // GRID_W
    i = pl.program_id(1)
    first = i == 0
    last = i == pl.num_programs(1) - 1

    _fill_conv_buffer(ubuf, zu_ref, zup_ref, zun_ref, tm, first, last)
    _conv_branches(ubuf, zt_ref, ybuf, csc_ref, ccf_ref, cfb_ref, lng_ref, lnb_ref, tm, w_sc)

    kvbuf[0:halo_kv, :] = zkvp_ref[0]
    kvbuf[halo_kv:halo_kv + tm, :] = zkv_ref[0]
    kvbuf[halo_kv + tm:2 * halo_kv + tm, :] = zkvn_ref[0]

    r0 = i * tile_rows
    n_keys = NA_ROWS * GRID_W
    q_off = 2 * w_sc
    g_off = 2 * w_sc + w_na

    def row_body(j, carry):
        r = r0 + j
        rs = jnp.clip(r - NA_ROWS // 2, 0, rows_total - NA_ROWS)
        kstart = pl.multiple_of((rs - r0 + halo_rows) * GRID_W, GRID_W)
        bstart = rs - r + (NA_ROWS - 1)
        par = bstart % 2
        bp = bstart // 2
        qrow = pl.multiple_of(j * GRID_W, GRID_W)
        for p in range(w_na // LANES):
            lanes = slice(p * LANES, (p + 1) * LANES)
            q2 = zt_ref[0, pl.ds(qrow, GRID_W), q_off + p * LANES:q_off + (p + 1) * LANES]
            qs = _split_heads(q2)
            kw = kvbuf[pl.ds(kstart, n_keys), lanes]
            vw = kvbuf[pl.ds(kstart, n_keys), w_na + p * LANES:w_na + (p + 1) * LANES]
            kc = kvc_ref[0, :, lanes]
            vc = kvc_ref[0, :, w_na + p * LANES:w_na + (p + 1) * LANES]
            s_w = lax.dot_general(qs, kw, _NT, preferred_element_type=F32)
            s_c = lax.dot_general(qs, kc, _NT, preferred_element_type=F32)
            bias = jnp.concatenate([bias_ref[par, p, bp + t] for t in range(NA_ROWS // 2)], axis=-1)
            s_w = s_w + bias
            m = jnp.maximum(jnp.max(s_w, axis=-1, keepdims=True), jnp.max(s_c, axis=-1, keepdims=True))
            p_w = jnp.exp(s_w - m)
            p_c = jnp.exp(s_c - m)
            den = jnp.sum(p_w, axis=-1, keepdims=True) + jnp.sum(p_c, axis=-1, keepdims=True)
            o2 = (jnp.dot(p_w.astype(BF16), vw, preferred_element_type=F32)
                  + jnp.dot(p_c.astype(BF16), vc, preferred_element_type=F32)) / den
            o = _merge_heads(o2, GRID_W)
            gate = zt_ref[0, pl.ds(qrow, GRID_W), g_off + p * LANES:g_off + (p + 1) * LANES].astype(F32)
            ybuf[pl.ds(qrow, GRID_W), w_sc + p * LANES:w_sc + (p + 1) * LANES] = (o * gate).astype(BF16)
        return carry

    lax.fori_loop(0, tile_rows, row_body, 0)

    y = jnp.dot(ybuf[...], wout_ref[...], preferred_element_type=F32)
    xn = x_ref[0] + gate_ref[0] * y
    if final:
        ms = jnp.mean(xn * xn, axis=-1, keepdims=True)
        xn = xn * lax.rsqrt(ms + NORM_EPS) * fg_ref[...]
    o_ref[0] = xn


def _ctx_mixer_kernel(x_ref, gate_ref, zt_ref, zu_ref, zkv_ref,
                      csc_ref, ccf_ref, cfb_ref, lng_ref, lnb_ref, wout_ref,
                      o_ref, ubuf, ybuf, *, tm, d):
    w_sc = d // 4
    w_na = d // 2
    _fill_conv_buffer(ubuf, zu_ref, None, None, tm, None, None)
    _conv_branches(ubuf, zt_ref, ybuf, csc_ref, ccf_ref, cfb_ref, lng_ref, lnb_ref, tm, w_sc)
    q_off = 2 * w_sc
    g_off = 2 * w_sc + w_na
    for p in range(w_na // LANES):
        lanes = slice(p * LANES, (p + 1) * LANES)
        qs = _split_heads(zt_ref[0, :, q_off + p * LANES:q_off + (p + 1) * LANES])
        kc = zkv_ref[0, :, lanes]
        vc = zkv_ref[0, :, w_na + p * LANES:w_na + (p + 1) * LANES]
        s = lax.dot_general(qs, kc, _NT, preferred_element_type=F32)
        m = jnp.max(s, axis=-1, keepdims=True)
        pr = jnp.exp(s - m)
        den = jnp.sum(pr, axis=-1, keepdims=True)
        o2 = jnp.dot(pr.astype(BF16), vc, preferred_element_type=F32) / den
        o = _merge_heads(o2, tm)
        gate = zt_ref[0, :, g_off + p * LANES:g_off + (p + 1) * LANES].astype(F32)
        ybuf[:, w_sc + p * LANES:w_sc + (p + 1) * LANES] = (o * gate).astype(BF16)
    y = jnp.dot(ybuf[...], wout_ref[...], preferred_element_type=F32)
    o_ref[0] = x_ref[0] + gate_ref[0] * y


def _latent_mixer(x, gate, zt, zu, zkv, zkv_c, bias, csc, ccf, cfb, lng, lnb, wout, fg, *, tm, final):
    b, n, d = x.shape
    n_ctx = zkv_c.shape[1]
    w_sc = d // 4
    halo_kv = (NA_ROWS // 2) * GRID_W
    nt = n // tm
    u_per = tm // HALO_U
    kv_per = tm // halo_kv
    const2 = lambda shape: pl.BlockSpec(shape, lambda bi, i: (0, 0))
    tile = lambda width: pl.BlockSpec((1, tm, width), lambda bi, i: (bi, i, 0))
    in_specs = [
        tile(d),
        pl.BlockSpec((1, 1, d), lambda bi, i: (bi, 0, 0)),
        tile(zt.shape[2]),
        tile(zu.shape[2]),
        pl.BlockSpec((1, HALO_U, zu.shape[2]), lambda bi, i: (bi, jnp.maximum(i * u_per - 1, 0), 0)),
        pl.BlockSpec((1, HALO_U, zu.shape[2]),
                     lambda bi, i: (bi, jnp.minimum((i + 1) * u_per, n // HALO_U - 1), 0)),
        tile(d),
        pl.BlockSpec((1, halo_kv, d), lambda bi, i: (bi, jnp.maximum(i * kv_per - 1, 0), 0)),
        pl.BlockSpec((1, halo_kv, d), lambda bi, i: (bi, jnp.minimum((i + 1) * kv_per, n // halo_kv - 1), 0)),
        pl.BlockSpec((1, n_ctx, d), lambda bi, i: (bi, 0, 0)),
        pl.BlockSpec(bias.shape, lambda bi, i: (0, 0, 0, 0, 0)),
        const2(csc.shape), const2(ccf.shape), const2(cfb.shape), const2(lng.shape), const2(lnb.shape),
        const2(wout.shape), const2(fg.shape),
    ]
    return pl.pallas_call(
        functools.partial(_latent_mixer_kernel, tm=tm, d=d, rows_total=n // GRID_W, final=final),
        grid=(b, nt),
        in_specs=in_specs,
        out_specs=tile(d),
        out_shape=jax.ShapeDtypeStruct((b, n, d), F32),
        scratch_shapes=[
            pltpu.VMEM((tm + 2 * HALO_U, 2 * w_sc), F32),
            pltpu.VMEM((tm + 2 * halo_kv, d), BF16),
            pltpu.VMEM((tm, d), BF16),
        ],
        input_output_aliases={0: 0},
        compiler_params=pltpu.CompilerParams(
            dimension_semantics=("parallel", "parallel"), vmem_limit_bytes=VMEM_LIMIT),
        name="latent_mixer",
    )(x, gate, zt, zu, zu, zu, zkv, zkv, zkv, zkv_c, bias, csc, ccf, cfb, lng, lnb, wout, fg)


def _ctx_mixer(x, gate, zt, zu, zkv, csc, ccf, cfb, lng, lnb, wout):
    b, n, d = x.shape
    w_sc = d // 4
    const2 = lambda shape: pl.BlockSpec(shape, lambda bi: (0, 0))
    tile = lambda width: pl.BlockSpec((1, n, width), lambda bi: (bi, 0, 0))
    return pl.pallas_call(
        functools.partial(_ctx_mixer_kernel, tm=n, d=d),
        grid=(b,),
        in_specs=[
            tile(d),
            pl.BlockSpec((1, 1, d), lambda bi: (0, 0, 0)),
            tile(zt.shape[2]), tile(zu.shape[2]), tile(d),
            const2(csc.shape), const2(ccf.shape), const2(cfb.shape), const2(lng.shape), const2(lnb.shape),
            const2(wout.shape),
        ],
        out_specs=tile(d),
        out_shape=jax.ShapeDtypeStruct((b, n, d), F32),
        scratch_shapes=[
            pltpu.VMEM((n + 2 * HALO_U, 2 * w_sc), F32),
            pltpu.VMEM((n, d), BF16),
        ],
        input_output_aliases={0: 0},
        compiler_params=pltpu.CompilerParams(
            dimension_semantics=("parallel",), vmem_limit_bytes=VMEM_LIMIT),
        name="ctx_mixer",
    )(x, gate, zt, zu, zkv, csc, ccf, cfb, lng, lnb, wout)


def _bias_tables(rpb):
    depth, heads = rpb.shape[0], rpb.shape[1]
    cols = np.arange(GRID_W)
    col_start = np.clip(cols - NA_COLS // 2, 0, GRID_W - NA_COLS)
    kc = np.arange(GRID_W)
    inside = (kc[None, :] >= col_start[:, None]) & (kc[None, :] < col_start[:, None] + NA_COLS)
    rel = np.clip(kc[None, :] - cols[:, None] + (NA_COLS - 1), 0, 2 * NA_COLS - 2)
    tab = rpb[:, :, :, rel]
    tab = jnp.where(jnp.asarray(inside)[None, None, None], tab, MASK_BIAS)
    tab = jnp.pad(tab, ((0, 0), (0, 0), (0, 1), (0, 0), (0, 0)))
    n_dp = NA_ROWS - 1
    parts = []
    for par in range(2):
        d0 = tab[:, :, par:par + 2 * n_dp:2]
        d1 = tab[:, :, par + 1:par + 1 + 2 * n_dp:2]
        blk = jnp.concatenate([d0, d1], axis=-1)
        blk = blk.reshape(depth, heads // 2, 2, n_dp, GRID_W, LANES)
        blk = blk.transpose(0, 1, 3, 2, 4, 5).reshape(depth, heads // 2, n_dp, 2 * GRID_W, LANES)
        parts.append(blk)
    return jnp.stack(parts, axis=1).astype(F32)


def kernel(x, c, ctx, c_ctx, norm_g, w_ada, b_ada, w_in, conv_sc, rpb, conv_cf, conv_cf_b,
           ln_cf_g, ln_cf_b, w_out, final_g):
    depth = w_in.shape[0]
    b, n, d = x.shape
    n_ctx = ctx.shape[1]
    mod_rows = 8 * ((b + 1 + 7) // 8)
    cc = jnp.concatenate([c, c_ctx[None, :], jnp.zeros((mod_rows - b - 1, d), F32)], axis=0)
    mods = _modulation(cc, w_ada, b_ada)
    w_in_b = w_in.astype(BF16)
    w_out_b = w_out.astype(BF16)
    bias = _bias_tables(rpb)
    fg = final_g.reshape(1, d)

    for l in range(depth):
        final = l == depth - 1
        shift = mods[l, :b, 0:d].reshape(b, 1, d)
        scale = mods[l, :b, d:2 * d].reshape(b, 1, d)
        gate = mods[l, :b, 2 * d:3 * d].reshape(b, 1, d)
        shift_c = mods[l, b:b + 1, 0:d].reshape(1, 1, d)
        scale_c = mods[l, b:b + 1, d:2 * d].reshape(1, 1, d)
        gate_c = mods[l, b:b + 1, 2 * d:3 * d].reshape(1, 1, d)
        g = norm_g[l].reshape(1, d)
        csc, ccf = conv_sc[l], conv_cf[l]
        cfb, lng, lnb = (v[l].reshape(1, -1) for v in (conv_cf_b, ln_cf_g, ln_cf_b))

        zt_c, zu_c, zkv_c = _in_proj(ctx, g, scale_c, shift_c, w_in_b[l], tm=n_ctx)
        zt, zu, zkv = _in_proj(x, g, scale, shift, w_in_b[l], tm=512)
        x = _latent_mixer(x, gate, zt, zu, zkv, zkv_c, bias[l], csc, ccf, cfb, lng, lnb, w_out_b[l], fg,
                          tm=512, final=final)
        if not final:
            ctx = _ctx_mixer(ctx, gate_c, zt_c, zu_c, zkv_c, csc, ccf, cfb, lng, lnb, w_out_b[l])
    return x
```

```python
import functools

import numpy as np
import jax
import jax.numpy as jnp
from jax import lax
from jax.experimental import pallas as pl
from jax.experimental.pallas import tpu as pltpu

F32 = jnp.float32
BF16 = jnp.bfloat16

GRID_W = 64
HEAD_DIM = 64
NA_ROWS = 8
NA_COLS = 16
SC_K = 3
CF_K = 31
NORM_EPS = 1e-6
LN_EPS = 1e-5
MASK_BIAS = -1e30

LANES = 128
SUBLANES = 8
HALO_U = 16
LOG2_E = 1.4426950408889634
VMEM_LIMIT = 56 * 1024 * 1024


def _silu(v):
    return v * (1.0 / (1.0 + jnp.exp(-v)))


def _sigmoid(v):
    return 1.0 / (1.0 + jnp.exp(-v))


def _mod_kernel(c_ref, w_ref, b_ref, o_ref):
    a = _silu(c_ref[...]).astype(BF16)
    o_ref[0] = jnp.dot(a, w_ref[0].astype(BF16), preferred_element_type=F32) + b_ref[0]


def _modulation(cc, w_ada, b_ada):
    depth, d, d3 = w_ada.shape
    rows = cc.shape[0]
    tn = 1024
    return pl.pallas_call(
        _mod_kernel,
        grid=(depth, d3 // tn),
        in_specs=[
            pl.BlockSpec((rows, d), lambda l, n: (0, 0)),
            pl.BlockSpec((1, d, tn), lambda l, n: (l, 0, n)),
            pl.BlockSpec((1, 1, tn), lambda l, n: (l, 0, n)),
        ],
        out_specs=pl.BlockSpec((1, rows, tn), lambda l, n: (l, 0, n)),
        out_shape=jax.ShapeDtypeStruct((depth, rows, d3), F32),
        compiler_params=pltpu.CompilerParams(vmem_limit_bytes=VMEM_LIMIT),
        name="adaln_mod",
    )(cc, w_ada, b_ada.reshape(depth, 1, d3))


def _in_proj_kernel(x_ref, g_ref, scale_ref, shift_ref, w_ref, zt_ref, zu_ref, zkv_ref, *, d):
    w_sc = d // 4
    w_na = d // 2
    x = x_ref[0]
    ms = jnp.mean(x * x, axis=-1, keepdims=True)
    mul = g_ref[...] * (1.0 + scale_ref[0])
    h = (x * lax.rsqrt(ms + NORM_EPS) * mul + shift_ref[0]).astype(BF16)

    def proj(start, width):
        return jnp.dot(h, w_ref[:, start:start + width], preferred_element_type=F32)

    o_na = 4 * w_sc
    o_cf = o_na + 4 * w_na
    zu_ref[0, :, 0:w_sc] = (proj(2 * w_sc, w_sc) * proj(0, w_sc)).astype(BF16)
    zt_ref[0, :, 0:w_sc] = (proj(w_sc, w_sc) * _silu(proj(3 * w_sc, w_sc))).astype(BF16)
    zu_ref[0, :, w_sc:2 * w_sc] = (proj(o_cf, w_sc) * _sigmoid(proj(o_cf + w_sc, w_sc))).astype(BF16)
    zt_ref[0, :, w_sc:2 * w_sc] = _silu(proj(o_cf + 2 * w_sc, w_sc)).astype(BF16)
    zt_ref[0, :, 2 * w_sc:2 * w_sc + w_na] = (proj(o_na, w_na) * (HEAD_DIM ** -0.5 * LOG2_E)).astype(BF16)
    zkv_ref[0, :, 0:w_na] = proj(o_na + w_na, w_na).astype(BF16)
    zkv_ref[0, :, w_na:2 * w_na] = proj(o_na + 2 * w_na, w_na).astype(BF16)
    zt_ref[0, :, 2 * w_sc + w_na:2 * w_sc + 2 * w_na] = _silu(proj(o_na + 3 * w_na, w_na)).astype(BF16)


def _in_proj(x, g, scale, shift, w, *, tm):
    b, n, d = x.shape
    d_proj = w.shape[1]
    sb = 0 if scale.shape[0] == 1 else 1
    vec = pl.BlockSpec((1, 1, d), lambda bi, i: (bi * sb, 0, 0))
    out = lambda width: pl.BlockSpec((1, tm, width), lambda bi, i: (bi, i, 0))
    return pl.pallas_call(
        functools.partial(_in_proj_kernel, d=d),
        grid=(b, n // tm),
        in_specs=[
            pl.BlockSpec((1, tm, d), lambda bi, i: (bi, i, 0)),
            pl.BlockSpec((1, d), lambda bi, i: (0, 0)),
            vec, vec,
            pl.BlockSpec((d, d_proj), lambda bi, i: (0, 0)),
        ],
        out_specs=[out(d + d // 2), out(d // 2), out(d)],
        out_shape=[
            jax.ShapeDtypeStruct((b, n, d + d // 2), BF16),
            jax.ShapeDtypeStruct((b, n, d // 2), BF16),
            jax.ShapeDtypeStruct((b, n, d), BF16),
        ],
        compiler_params=pltpu.CompilerParams(
            dimension_semantics=("parallel", "parallel"), vmem_limit_bytes=VMEM_LIMIT),
        name="in_proj",
    )(x, g, scale, shift, w)


def _fill_conv_buffer(ubuf, zu_ref, prev_ref, next_ref, tm, first, last):
    width = ubuf.shape[1]
    zeros = jnp.zeros((HALO_U, width), F32)
    if prev_ref is None:
        ubuf[0:HALO_U, :] = zeros
        ubuf[HALO_U + tm:2 * HALO_U + tm, :] = zeros
    else:
        ubuf[0:HALO_U, :] = jnp.where(first, zeros, prev_ref[0].astype(F32))
        ubuf[HALO_U + tm:2 * HALO_U + tm, :] = jnp.where(last, zeros, next_ref[0].astype(F32))
    ubuf[HALO_U:HALO_U + tm, :] = zu_ref[0].astype(F32)


def _dwconv_chunk(ubuf, w_ref, base, rows, lane0, width, ksize):
    out = None
    for shift in range(SUBLANES):
        acc = None
        for j in range(ksize):
            offset = j - ksize // 2
            if offset % SUBLANES != shift:
                continue
            start = HALO_U + base + offset - shift
            term = ubuf[start:start + rows + SUBLANES, lane0:lane0 + width] * w_ref[j:j + 1, :]
            acc = term if acc is None else acc + term
        if acc is None:
            continue
        part = acc[shift:shift + rows]
        out = part if out is None else out + part
    return out


def _conv_branches(ubuf, zt_ref, ybuf, csc_ref, ccf_ref, cfb_ref, lng_ref, lnb_ref, tm, w_sc):
    chunk = 64
    for c0 in range(0, tm, chunk):
        ya = _dwconv_chunk(ubuf, csc_ref, c0, chunk, 0, w_sc, SC_K)
        ya = ya * zt_ref[0, c0:c0 + chunk, 0:w_sc].astype(F32)
        ybuf[c0:c0 + chunk, 0:w_sc] = ya.astype(BF16)
        u = _dwconv_chunk(ubuf, ccf_ref, c0, chunk, w_sc, w_sc, CF_K) + cfb_ref[...]
        mu = jnp.mean(u, axis=-1, keepdims=True)
        uc = u - mu
        var = jnp.mean(uc * uc, axis=-1, keepdims=True)
        un = uc * lax.rsqrt(var + LN_EPS) * lng_ref[...] + lnb_ref[...]
        yc = _silu(un) * zt_ref[0, c0:c0 + chunk, w_sc:2 * w_sc].astype(F32)
        ybuf[c0:c0 + chunk, 3 * w_sc:4 * w_sc] = yc.astype(BF16)


def _split_heads(q2):
    lane = lax.broadcasted_iota(jnp.int32, q2.shape, 1)
    zero = jnp.zeros_like(q2)
    return jnp.concatenate([jnp.where(lane < HEAD_DIM, q2, zero),
                            jnp.where(lane >= HEAD_DIM, q2, zero)], axis=0)


def _merge_heads(o2, rows):
    lane = lax.broadcasted_iota(jnp.int32, (rows, LANES), 1)
    return jnp.where(lane < HEAD_DIM, o2[0:rows], o2[rows:2 * rows])


def _lane_tiles(a):
    return [a[:, i:i + LANES] for i in range(0, a.shape[1], LANES)]


_NT = (((1,), (1,)), ((), ()))


def _latent_mixer_kernel(x_ref, gate_ref, zt_ref, zu_ref, zup_ref, zun_ref, zkv_ref, zkvp_ref, zkvn_ref,
                         kvc_ref, bias_ref, csc_ref, ccf_ref, cfb_ref, lng_ref, lnb_ref, wout_ref, fg_ref,
                         o_ref, ubuf, kvbuf, ybuf, *, tm, d, rows_total, final):
    w_sc = d // 4
    w_na = d // 2
    halo_kv = zkvp_ref.shape[1]
    halo_rows = halo_kv // GRID_W
    tile_rows = tm // GRID_W
    i = pl.program_id(1)
    first = i == 0
    last = i == pl.num_programs(1) - 1

    _fill_conv_buffer(ubuf, zu_ref, zup_ref, zun_ref, tm, first, last)
    _conv_branches(ubuf, zt_ref, ybuf, csc_ref, ccf_ref, cfb_ref, lng_ref, lnb_ref, tm, w_sc)

    kvbuf[0:halo_kv, :] = zkvp_ref[0]
    kvbuf[halo_kv:halo_kv + tm, :] = zkv_ref[0]
    kvbuf[halo_kv + tm:2 * halo_kv + tm, :] = zkvn_ref[0]

    r0 = i * tile_rows
    n_keys = NA_ROWS * GRID_W
    q_off = 2 * w_sc
    g_off = 2 * w_sc + w_na

    n_pairs = w_na // LANES
    row_params = []
    for j in range(tile_rows):
        r = r0 + j
        rs = jnp.clip(r - NA_ROWS // 2, 0, rows_total - NA_ROWS)
        kstart = pl.multiple_of((rs - r0 + halo_rows) * GRID_W, GRID_W)
        bstart = rs - r + (NA_ROWS - 1)
        row_params.append((kstart, bstart % 2, bstart // 2))

    def scores(t):
        j, p = divmod(t, n_pairs)
        kstart, par, bp = row_params[j]
        lanes = slice(p * LANES, (p + 1) * LANES)
        q2 = zt_ref[0, j * GRID_W:(j + 1) * GRID_W, q_off + p * LANES:q_off + (p + 1) * LANES]
        qs = _split_heads(q2)
        kw = kvbuf[pl.ds(kstart, n_keys), lanes]
        kc = kvc_ref[0, :, lanes]
        s_w = lax.dot_general(qs, kw, _NT, preferred_element_type=F32)
        s_c = lax.dot_general(qs, kc, _NT, preferred_element_type=F32)
        bias = jnp.concatenate([bias_ref[par, p, bp + u] for u in range(NA_ROWS // 2)], axis=-1)
        s_w = s_w + bias
        m = jnp.max(functools.reduce(jnp.maximum, _lane_tiles(s_w) + _lane_tiles(s_c)), axis=-1, keepdims=True)
        return s_w, s_c, m

    def probs(s_w, s_c, m):
        p_w = jnp.exp2(s_w - m)
        p_c = jnp.exp2(s_c - m)
        den = jnp.sum(functools.reduce(jnp.add, _lane_tiles(p_w) + _lane_tiles(p_c)), axis=-1, keepdims=True)
        return p_w.astype(BF16), p_c.astype(BF16), den

    def outputs(t, p_w, p_c, den):
        j, p = divmod(t, n_pairs)
        kstart = row_params[j][0]
        vw = kvbuf[pl.ds(kstart, n_keys), w_na + p * LANES:w_na + (p + 1) * LANES]
        vc = kvc_ref[0, :, w_na + p * LANES:w_na + (p + 1) * LANES]
        o2 = (jnp.dot(p_w, vw, preferred_element_type=F32)
              + jnp.dot(p_c, vc, preferred_element_type=F32)) / den
        o = _merge_heads(o2, GRID_W)
        rows = slice(j * GRID_W, (j + 1) * GRID_W)
        gate = zt_ref[0, rows, g_off + p * LANES:g_off + (p + 1) * LANES].astype(F32)
        ybuf[rows, w_sc + p * LANES:w_sc + (p + 1) * LANES] = (o * gate).astype(BF16)

    n_items = tile_rows * n_pairs
    st_scores, st_probs = {}, {}
    for k in range(n_items + 2):
        if k < n_items:
            st_scores[k] = scores(k)
        if 0 <= k - 1 < n_items:
            st_probs[k - 1] = probs(*st_scores.pop(k - 1))
        if 0 <= k - 2 < n_items:
            outputs(k - 2, *st_probs.pop(k - 2))

    y = jnp.dot(ybuf[...], wout_ref[...], preferred_element_type=F32)
    xn = x_ref[0] + gate_ref[0] * y
    if final:
        ms = jnp.mean(xn * xn, axis=-1, keepdims=True)
        xn = xn * lax.rsqrt(ms + NORM_EPS) * fg_ref[...]
    o_ref[0] = xn


def _ctx_mixer_kernel(x_ref, gate_ref, zt_ref, zu_ref, zkv_ref,
                      csc_ref, ccf_ref, cfb_ref, lng_ref, lnb_ref, wout_ref,
                      o_ref, ubuf, ybuf, *, tm, d):
    w_sc = d // 4
    w_na = d // 2
    _fill_conv_buffer(ubuf, zu_ref, None, None, tm, None, None)
    _conv_branches(ubuf, zt_ref, ybuf, csc_ref, ccf_ref, cfb_ref, lng_ref, lnb_ref, tm, w_sc)
    q_off = 2 * w_sc
    g_off = 2 * w_sc + w_na
    for p in range(w_na // LANES):
        lanes = slice(p * LANES, (p + 1) * LANES)
        qs = _split_heads(zt_ref[0, :, q_off + p * LANES:q_off + (p + 1) * LANES])
        kc = zkv_ref[0, :, lanes]
        vc = zkv_ref[0, :, w_na + p * LANES:w_na + (p + 1) * LANES]
        s = lax.dot_general(qs, kc, _NT, preferred_element_type=F32)
        m = jnp.max(s, axis=-1, keepdims=True)
        pr = jnp.exp2(s - m)
        den = jnp.sum(pr, axis=-1, keepdims=True)
        o2 = jnp.dot(pr.astype(BF16), vc, preferred_element_type=F32) / den
        o = _merge_heads(o2, tm)
        gate = zt_ref[0, :, g_off + p * LANES:g_off + (p + 1) * LANES].astype(F32)
        ybuf[:, w_sc + p * LANES:w_sc + (p + 1) * LANES] = (o * gate).astype(BF16)
    y = jnp.dot(ybuf[...], wout_ref[...], preferred_element_type=F32)
    o_ref[0] = x_ref[0] + gate_ref[0] * y


def _latent_mixer(x, gate, zt, zu, zkv, zkv_c, bias, csc, ccf, cfb, lng, lnb, wout, fg, *, tm, final):
    b, n, d = x.shape
    n_ctx = zkv_c.shape[1]
    w_sc = d // 4
    halo_kv = (NA_ROWS // 2) * GRID_W
    nt = n // tm
    u_per = tm // HALO_U
    kv_per = tm // halo_kv
    const2 = lambda shape: pl.BlockSpec(shape, lambda bi, i: (0, 0))
    tile = lambda width: pl.BlockSpec((1, tm, width), lambda bi, i: (bi, i, 0))
    in_specs = [
        tile(d),
        pl.BlockSpec((1, 1, d), lambda bi, i: (bi, 0, 0)),
        tile(zt.shape[2]),
        tile(zu.shape[2]),
        pl.BlockSpec((1, HALO_U, zu.shape[2]), lambda bi, i: (bi, jnp.maximum(i * u_per - 1, 0), 0)),
        pl.BlockSpec((1, HALO_U, zu.shape[2]),
                     lambda bi, i: (bi, jnp.minimum((i + 1) * u_per, n // HALO_U - 1), 0)),
        tile(d),
        pl.BlockSpec((1, halo_kv, d), lambda bi, i: (bi, jnp.maximum(i * kv_per - 1, 0), 0)),
        pl.BlockSpec((1, halo_kv, d), lambda bi, i: (bi, jnp.minimum((i + 1) * kv_per, n // halo_kv - 1), 0)),
        pl.BlockSpec((1, n_ctx, d), lambda bi, i: (bi, 0, 0)),
        pl.BlockSpec(bias.shape, lambda bi, i: (0, 0, 0, 0, 0)),
        const2(csc.shape), const2(ccf.shape), const2(cfb.shape), const2(lng.shape), const2(lnb.shape),
        const2(wout.shape), const2(fg.shape),
    ]
    return pl.pallas_call(
        functools.partial(_latent_mixer_kernel, tm=tm, d=d, rows_total=n // GRID_W, final=final),
        grid=(b, nt),
        in_specs=in_specs,
        out_specs=tile(d),
        out_shape=jax.ShapeDtypeStruct((b, n, d), F32),
        scratch_shapes=[
            pltpu.VMEM((tm + 2 * HALO_U, 2 * w_sc), F32),
            pltpu.VMEM((tm + 2 * halo_kv, d), BF16),
            pltpu.VMEM((tm, d), BF16),
        ],
        compiler_params=pltpu.CompilerParams(
            dimension_semantics=("parallel", "parallel"), vmem_limit_bytes=VMEM_LIMIT),
        name="latent_mixer",
    )(x, gate, zt, zu, zu, zu, zkv, zkv, zkv, zkv_c, bias, csc, ccf, cfb, lng, lnb, wout, fg)


def _ctx_mixer(x, gate, zt, zu, zkv, csc, ccf, cfb, lng, lnb, wout):
    b, n, d = x.shape
    w_sc = d // 4
    const2 = lambda shape: pl.BlockSpec(shape, lambda bi: (0, 0))
    tile = lambda width: pl.BlockSpec((1, n, width), lambda bi: (bi, 0, 0))
    return pl.pallas_call(
        functools.partial(_ctx_mixer_kernel, tm=n, d=d),
        grid=(b,),
        in_specs=[
            tile(d),
            pl.BlockSpec((1, 1, d), lambda bi: (0, 0, 0)),
            tile(zt.shape[2]), tile(zu.shape[2]), tile(d),
            const2(csc.shape), const2(ccf.shape), const2(cfb.shape), const2(lng.shape), const2(lnb.shape),
            const2(wout.shape),
        ],
        out_specs=tile(d),
        out_shape=jax.ShapeDtypeStruct((b, n, d), F32),
        scratch_shapes=[
            pltpu.VMEM((n + 2 * HALO_U, 2 * w_sc), F32),
            pltpu.VMEM((n, d), BF16),
        ],
        compiler_params=pltpu.CompilerParams(
            dimension_semantics=("parallel",), vmem_limit_bytes=VMEM_LIMIT),
        name="ctx_mixer",
    )(x, gate, zt, zu, zkv, csc, ccf, cfb, lng, lnb, wout)


def _bias_tables(rpb):
    depth, heads = rpb.shape[0], rpb.shape[1]
    cols = np.arange(GRID_W)
    col_start = np.clip(cols - NA_COLS // 2, 0, GRID_W - NA_COLS)
    kc = np.arange(GRID_W)
    inside = (kc[None, :] >= col_start[:, None]) & (kc[None, :] < col_start[:, None] + NA_COLS)
    rel = np.clip(kc[None, :] - cols[:, None] + (NA_COLS - 1), 0, 2 * NA_COLS - 2)
    tab = rpb[:, :, :, rel] * LOG2_E
    tab = jnp.where(jnp.asarray(inside)[None, None, None], tab, MASK_BIAS)
    tab = jnp.pad(tab, ((0, 0), (0, 0), (0, 1), (0, 0), (0, 0)))
    n_dp = NA_ROWS - 1
    parts = []
    for par in range(2):
        d0 = tab[:, :, par:par + 2 * n_dp:2]
        d1 = tab[:, :, par + 1:par + 1 + 2 * n_dp:2]
        blk = jnp.concatenate([d0, d1], axis=-1)
        blk = blk.reshape(depth, heads // 2, 2, n_dp, GRID_W, LANES)
        blk = blk.transpose(0, 1, 3, 2, 4, 5).reshape(depth, heads // 2, n_dp, 2 * GRID_W, LANES)
        parts.append(blk)
    return jnp.stack(parts, axis=1).astype(F32)


def kernel(x, c, ctx, c_ctx, norm_g, w_ada, b_ada, w_in, conv_sc, rpb, conv_cf, conv_cf_b,
           ln_cf_g, ln_cf_b, w_out, final_g):
    depth = w_in.shape[0]
    b, n, d = x.shape
    n_ctx = ctx.shape[1]
    mod_rows = 8 * ((b + 1 + 7) // 8)
    cc = jnp.concatenate([c, c_ctx[None, :], jnp.zeros((mod_rows - b - 1, d), F32)], axis=0)
    mods = _modulation(cc, w_ada, b_ada)
    w_in_b = w_in.astype(BF16)
    w_out_b = w_out.astype(BF16)
    bias = _bias_tables(rpb)
    fg = final_g.reshape(1, d)

    for l in range(depth):
        final = l == depth - 1
        shift = mods[l, :b, 0:d].reshape(b, 1, d)
        scale = mods[l, :b, d:2 * d].reshape(b, 1, d)
        gate = mods[l, :b, 2 * d:3 * d].reshape(b, 1, d)
        shift_c = mods[l, b:b + 1, 0:d].reshape(1, 1, d)
        scale_c = mods[l, b:b + 1, d:2 * d].reshape(1, 1, d)
        gate_c = mods[l, b:b + 1, 2 * d:3 * d].reshape(1, 1, d)
        g = norm_g[l].reshape(1, d)
        csc, ccf = conv_sc[l], conv_cf[l]
        cfb, lng, lnb = (v[l].reshape(1, -1) for v in (conv_cf_b, ln_cf_g, ln_cf_b))

        zt_c, zu_c, zkv_c = _in_proj(ctx, g, scale_c, shift_c, w_in_b[l], tm=n_ctx)
        zt, zu, zkv = _in_proj(x, g, scale, shift, w_in_b[l], tm=512)
        x = _latent_mixer(x, gate, zt, zu, zkv, zkv_c, bias[l], csc, ccf, cfb, lng, lnb, w_out_b[l], fg,
                          tm=512, final=final)
        if not final:
            ctx = _ctx_mixer(ctx, gate_c, zt_c, zu_c, zkv_c, csc, ccf, cfb, lng, lnb, w_out_b[l])
    return x
```

```python
import functools

import numpy as np
import jax
import jax.numpy as jnp
from jax import lax
from jax.experimental import pallas as pl
from jax.experimental.pallas import tpu as pltpu

F32 = jnp.float32
BF16 = jnp.bfloat16

GRID_W = 64
HEAD_DIM = 64
NA_ROWS = 8
NA_COLS = 16
SC_K = 3
CF_K = 31
NORM_EPS = 1e-6
LN_EPS = 1e-5
MASK_BIAS = -1e30

LANES = 128
SUBLANES = 8
MXU_N = 256
HALO_U = 16
LOG2_E = 1.4426950408889634
VMEM_LIMIT = 56 * 1024 * 1024
TM_LATENT = 512


def _silu(v):
    return v * (1.0 / (1.0 + jnp.exp(-v)))


def _sigmoid(v):
    return 1.0 / (1.0 + jnp.exp(-v))


def _mod_kernel(c_ref, w_ref, b_ref, o_ref):
    a = _silu(c_ref[...]).astype(BF16)
    o_ref[0] = jnp.dot(a, w_ref[0].astype(BF16), preferred_element_type=F32) + b_ref[0]


def _modulation(cc, w_ada, b_ada):
    depth, d, d3 = w_ada.shape
    rows = cc.shape[0]
    tn = 1024
    return pl.pallas_call(
        _mod_kernel,
        grid=(depth, d3 // tn),
        in_specs=[
            pl.BlockSpec((rows, d), lambda l, n: (0, 0)),
            pl.BlockSpec((1, d, tn), lambda l, n: (l, 0, n)),
            pl.BlockSpec((1, 1, tn), lambda l, n: (l, 0, n)),
        ],
        out_specs=pl.BlockSpec((1, rows, tn), lambda l, n: (l, 0, n)),
        out_shape=jax.ShapeDtypeStruct((depth, rows, d3), F32),
        compiler_params=pltpu.CompilerParams(vmem_limit_bytes=VMEM_LIMIT),
        name="adaln_mod",
    )(cc, w_ada, b_ada.reshape(depth, 1, d3))


class _DepthwiseConv:
    def __init__(self, ubuf, w_ref, lane0, w_lane0, width, ksize):
        self.ubuf, self.w_ref, self.lanes, self.ksize = ubuf, w_ref, slice(lane0, lane0 + width), ksize
        self.w_lanes = slice(w_lane0, w_lane0 + width)
        self.shifts = sorted({(j - ksize // 2) % SUBLANES for j in range(ksize)})
        self.rotated = {}
        self.row = lax.broadcasted_iota(jnp.int32, (SUBLANES, width), 0)

    def _rotated_partial(self, shift, row_tile):
        key = (shift, row_tile)
        if key not in self.rotated:
            acc = None
            for j in range(self.ksize):
                offset = j - self.ksize // 2
                if offset % SUBLANES != shift:
                    continue
                start = HALO_U + row_tile * SUBLANES + offset - shift
                term = self.ubuf[start:start + SUBLANES, self.lanes] * self.w_ref[j:j + 1, self.w_lanes]
                acc = term if acc is None else acc + term
            self.rotated[key] = acc if shift == 0 else pltpu.roll(acc, SUBLANES - shift, 0)
        return self.rotated[key]

    def tile(self, row_tile):
        out = None
        for shift in self.shifts:
            part = self._rotated_partial(shift, row_tile)
            if shift:
                part = jnp.where(self.row < SUBLANES - shift, part, self._rotated_partial(shift, row_tile + 1))
            out = part if out is None else out + part
        self.rotated = {k: v for k, v in self.rotated.items() if k[1] > row_tile}
        return out

    def rows(self, base, rows):
        return jnp.concatenate([self.tile(base // SUBLANES + k) for k in range(rows // SUBLANES)], axis=0)


def _conv_units(ubuf, gbuf, zc_ref, csc_ref, ccf_ref, cfb_ref, lng_ref, lnb_ref, tm, w_sc):
    chunk = 2 * SUBLANES
    conv_sc = [_DepthwiseConv(ubuf, csc_ref, l, l, LANES, SC_K) for l in range(0, w_sc, LANES)]
    conv_cf = [_DepthwiseConv(ubuf, ccf_ref, w_sc + l, l, LANES, CF_K) for l in range(0, w_sc, LANES)]

    def unit(c0):
        rows = slice(c0, c0 + chunk)
        ya = jnp.concatenate([cv.rows(c0, chunk) for cv in conv_sc], axis=-1) * gbuf[rows, 0:w_sc]
        zc_ref[0, rows, 0:w_sc] = ya.astype(BF16)
        u = jnp.concatenate([cv.rows(c0, chunk) for cv in conv_cf], axis=-1) + cfb_ref[...]
        mu = jnp.mean(u, axis=-1, keepdims=True)
        uc = u - mu
        var = jnp.mean(uc * uc, axis=-1, keepdims=True)
        un = uc * lax.rsqrt(var + LN_EPS) * lng_ref[...] + lnb_ref[...]
        yc = _silu(un) * gbuf[rows, w_sc:2 * w_sc]
        zc_ref[0, rows, w_sc:2 * w_sc] = yc.astype(BF16)

    return [functools.partial(unit, c0) for c0 in range(0, tm, chunk)]


def _proj_kernel(*refs, d, tm, has_halo):
    if has_halo:
        x_ref, xp_ref, xn_ref = refs[:3]
        refs = refs[3:]
    else:
        x_ref, refs = refs[0], refs[1:]
    (g_ref, scale_ref, shift_ref, w_ref, csc_ref, ccf_ref, cfb_ref, lng_ref, lnb_ref,
     zq_ref, zc_ref, zkv_ref, hbuf, ubuf, gbuf) = refs
    w_sc = d // 4
    w_na = d // 2
    o_na = 4 * w_sc
    o_cf = o_na + 4 * w_na
    mul = g_ref[...] * (1.0 + scale_ref[0])

    def norm(x):
        ms = jnp.mean(x * x, axis=-1, keepdims=True)
        return (x * lax.rsqrt(ms + NORM_EPS) * mul + shift_ref[0]).astype(BF16)

    half = tm // 2
    for r in (0, half):
        hbuf[HALO_U + r:HALO_U + r + half, :] = norm(x_ref[0, r:r + half, :])
    tile = slice(HALO_U, HALO_U + tm)
    if has_halo:
        hbuf[0:HALO_U, :] = norm(xp_ref[0])
        hbuf[HALO_U + tm:2 * HALO_U + tm, :] = norm(xn_ref[0])
        wide = slice(0, tm + 2 * HALO_U)
        first = pl.program_id(1) == 0
        last = pl.program_id(1) == pl.num_programs(1) - 1
    else:
        wide = tile

    def proj(start, rows=tile):
        return jnp.dot(hbuf[rows, :], w_ref[:, start:start + MXU_N], preferred_element_type=F32)

    u = jnp.concatenate([proj(2 * w_sc, wide) * proj(0, wide),
                         proj(o_cf, wide) * _sigmoid(proj(o_cf + w_sc, wide))], axis=-1)
    zeros = jnp.zeros((HALO_U, 2 * w_sc), F32)
    if has_halo:
        ubuf[0:HALO_U, :] = jnp.where(first, zeros, u[0:HALO_U])
        ubuf[tile, :] = u[HALO_U:HALO_U + tm]
        ubuf[HALO_U + tm:2 * HALO_U + tm, :] = jnp.where(last, zeros, u[HALO_U + tm:2 * HALO_U + tm])
    else:
        ubuf[0:HALO_U, :] = zeros
        ubuf[tile, :] = u
        ubuf[HALO_U + tm:2 * HALO_U + tm, :] = zeros
    gbuf[:, 0:w_sc] = proj(w_sc) * _silu(proj(3 * w_sc))
    gbuf[:, w_sc:2 * w_sc] = _silu(proj(o_cf + 2 * w_sc))

    def store(ref, col, value):
        ref[0, :, col:col + MXU_N] = value.astype(BF16)

    q_scale = HEAD_DIM ** -0.5 * LOG2_E
    attn_units = []
    for c in range(0, w_na, MXU_N):
        attn_units += [
            lambda c=c: store(zq_ref, c, proj(o_na + c) * q_scale),
            lambda c=c: store(zkv_ref, c, proj(o_na + w_na + c)),
            lambda c=c: store(zkv_ref, w_na + c, proj(o_na + 2 * w_na + c)),
            lambda c=c: store(zq_ref, w_na + c, _silu(proj(o_na + 3 * w_na + c))),
        ]
    conv_units = _conv_units(ubuf, gbuf, zc_ref, csc_ref, ccf_ref, cfb_ref, lng_ref, lnb_ref, tm, w_sc)
    per = max(len(conv_units) // len(attn_units), 1)
    for k, conv_unit in enumerate(conv_units):
        if k % per == 0 and attn_units:
            attn_units.pop(0)()
        conv_unit()
    for attn_unit in attn_units:
        attn_unit()


def _proj(x, g, scale, shift, w, csc, ccf, cfb, lng, lnb, *, tm):
    b, n, d = x.shape
    d_proj = w.shape[1]
    w_sc = d // 4
    nt = n // tm
    has_halo = nt > 1
    sb = 0 if scale.shape[0] == 1 else 1
    per = tm // HALO_U

    def const(shape):
        return pl.BlockSpec(shape, lambda bi, i: (0,) * len(shape), pipeline_mode=pl.Buffered(1))

    vec = pl.BlockSpec((1, 1, d), lambda bi, i: (bi * sb, 0, 0))
    out = lambda width: pl.BlockSpec((1, tm, width), lambda bi, i: (bi, i, 0))
    x_specs = [pl.BlockSpec((1, tm, d), lambda bi, i: (bi, i, 0))]
    x_args = [x]
    if has_halo:
        x_specs += [
            pl.BlockSpec((1, HALO_U, d), lambda bi, i: (bi, jnp.maximum(i * per - 1, 0), 0)),
            pl.BlockSpec((1, HALO_U, d), lambda bi, i: (bi, jnp.minimum((i + 1) * per, n // HALO_U - 1), 0)),
        ]
        x_args += [x, x]
    return pl.pallas_call(
        functools.partial(_proj_kernel, d=d, tm=tm, has_halo=has_halo),
        grid=(b, nt),
        in_specs=x_specs + [
            const((1, d)), vec, vec, const((d, d_proj)),
            const(csc.shape), const(ccf.shape), const(cfb.shape), const(lng.shape), const(lnb.shape),
        ],
        out_specs=[out(d), out(d // 2), out(d)],
        out_shape=[
            jax.ShapeDtypeStruct((b, n, d), BF16),
            jax.ShapeDtypeStruct((b, n, d // 2), BF16),
            jax.ShapeDtypeStruct((b, n, d), BF16),
        ],
        scratch_shapes=[
            pltpu.VMEM((tm + 2 * HALO_U, d), BF16),
            pltpu.VMEM((tm + 2 * HALO_U, 2 * w_sc), F32),
            pltpu.VMEM((tm, 2 * w_sc), F32),
        ],
        compiler_params=pltpu.CompilerParams(
            dimension_semantics=("parallel", "parallel"), vmem_limit_bytes=VMEM_LIMIT),
        name="proj_conv",
    )(*x_args, g, scale, shift, w, csc, ccf, cfb, lng, lnb)


def _split_heads(q2):
    lane = lax.broadcasted_iota(jnp.int32, q2.shape, 1)
    zero = jnp.zeros_like(q2)
    return jnp.concatenate([jnp.where(lane < HEAD_DIM, q2, zero),
                            jnp.where(lane >= HEAD_DIM, q2, zero)], axis=0)


def _merge_heads(o2, rows):
    lane = lax.broadcasted_iota(jnp.int32, (rows, LANES), 1)
    return jnp.where(lane < HEAD_DIM, o2[0:rows], o2[rows:2 * rows])


def _lane_tiles(a):
    return [a[:, i:i + LANES] for i in range(0, a.shape[1], LANES)]


def _fill_value_slabs(dst, rows, src_ref, n_pairs, v_lane0, dst_lane0):
    ones = jnp.ones((rows.stop - rows.start, LANES), BF16)
    for p in range(n_pairs):
        dst[rows, dst_lane0 + p * MXU_N:dst_lane0 + p * MXU_N + LANES] = (
            src_ref[0, :, v_lane0 + p * LANES:v_lane0 + (p + 1) * LANES])
        dst[rows, dst_lane0 + p * MXU_N + LANES:dst_lane0 + (p + 1) * MXU_N] = ones


_NT = (((1,), (1,)), ((), ()))


def _residual_out(x_ref, gate_ref, zc_ref, ybuf, woutc_ref, wouta_ref):
    y = (jnp.dot(zc_ref[0], woutc_ref[...], preferred_element_type=F32)
         + jnp.dot(ybuf[...], wouta_ref[...], preferred_element_type=F32))
    return x_ref[0] + gate_ref[0] * y


def _latent_mixer_kernel(x_ref, gate_ref, zq_ref, zc_ref, zkv_ref, zkvp_ref, zkvn_ref,
                         kvc_ref, bias_ref, woutc_ref, wouta_ref, fg_ref,
                         o_ref, kvbuf, vcbuf, ybuf, *, tm, d, rows_total, final):
    w_na = d // 2
    n_pairs = w_na // LANES
    halo_kv = zkvp_ref.shape[1]
    halo_rows = halo_kv // GRID_W
    tile_rows = tm // GRID_W
    i = pl.program_id(1)

    for src, row0 in ((zkvp_ref, 0), (zkv_ref, halo_kv), (zkvn_ref, halo_kv + tm)):
        rows = slice(row0, row0 + src.shape[1])
        kvbuf[rows, 0:w_na] = src[0, :, 0:w_na]
        _fill_value_slabs(kvbuf, rows, src, n_pairs, w_na, w_na)
    _fill_value_slabs(vcbuf, slice(0, vcbuf.shape[0]), kvc_ref, n_pairs, w_na, 0)

    r0 = i * tile_rows
    n_keys = NA_ROWS * GRID_W

    row_params = []
    for j in range(tile_rows):
        r = r0 + j
        rs = jnp.clip(r - NA_ROWS // 2, 0, rows_total - NA_ROWS)
        kstart = pl.multiple_of((rs - r0 + halo_rows) * GRID_W, GRID_W)
        row_params.append((kstart, rs - r + (NA_ROWS - 1)))

    def scores(it):
        j, p = divmod(it, n_pairs)
        kstart, bstart = row_params[j]
        lanes = slice(p * LANES, (p + 1) * LANES)
        qs = _split_heads(zq_ref[0, j * GRID_W:(j + 1) * GRID_W, lanes])
        kw = kvbuf[pl.ds(kstart, n_keys), lanes]
        kc = kvc_ref[0, :, lanes]
        s_w = lax.dot_general(qs, kw, _NT, preferred_element_type=F32)
        s_c = lax.dot_general(qs, kc, _NT, preferred_element_type=F32)
        bias = jnp.concatenate(
            [jnp.concatenate([bias_ref[2 * p, bstart + 2 * u], bias_ref[2 * p + 1, bstart + 2 * u]], axis=0)
             for u in range(NA_ROWS // 2)], axis=-1)
        s_w = s_w + bias
        m = jnp.max(functools.reduce(jnp.maximum, _lane_tiles(s_w) + _lane_tiles(s_c)), axis=-1, keepdims=True)
        return s_w, s_c, m

    def probs(s_w, s_c, m):
        return jnp.exp2(s_w - m).astype(BF16), jnp.exp2(s_c - m).astype(BF16)

    def outputs(it, p_w, p_c):
        j, p = divmod(it, n_pairs)
        kstart = row_params[j][0]
        vw = kvbuf[pl.ds(kstart, n_keys), w_na + p * MXU_N:w_na + (p + 1) * MXU_N]
        vc = vcbuf[:, p * MXU_N:(p + 1) * MXU_N]
        oa = jnp.dot(p_w, vw, preferred_element_type=F32) + jnp.dot(p_c, vc, preferred_element_type=F32)
        o = _merge_heads(oa[:, 0:LANES] / oa[:, LANES:MXU_N], GRID_W)
        rows = slice(j * GRID_W, (j + 1) * GRID_W)
        gate = zq_ref[0, rows, w_na + p * LANES:w_na + (p + 1) * LANES].astype(F32)
        ybuf[rows, p * LANES:(p + 1) * LANES] = (o * gate).astype(BF16)

    n_items = tile_rows * n_pairs
    st_scores, st_probs = {}, {}
    for k in range(n_items + 2):
        if k < n_items:
            st_scores[k] = scores(k)
        if 0 <= k - 1 < n_items:
            st_probs[k - 1] = probs(*st_scores.pop(k - 1))
        if 0 <= k - 2 < n_items:
            outputs(k - 2, *st_probs.pop(k - 2))

    xn = _residual_out(x_ref, gate_ref, zc_ref, ybuf, woutc_ref, wouta_ref)
    if final:
        ms = jnp.mean(xn * xn, axis=-1, keepdims=True)
        xn = xn * lax.rsqrt(ms + NORM_EPS) * fg_ref[...]
    o_ref[0] = xn


def _ctx_mixer_kernel(x_ref, gate_ref, zq_ref, zc_ref, zkv_ref, woutc_ref, wouta_ref,
                      o_ref, ybuf, *, tm, d):
    w_na = d // 2
    for p in range(w_na // LANES):
        lanes = slice(p * LANES, (p + 1) * LANES)
        qs = _split_heads(zq_ref[0, :, lanes])
        kc = zkv_ref[0, :, lanes]
        vc = zkv_ref[0, :, w_na + p * LANES:w_na + (p + 1) * LANES]
        s = lax.dot_general(qs, kc, _NT, preferred_element_type=F32)
        m = jnp.max(s, axis=-1, keepdims=True)
        pr = jnp.exp2(s - m)
        den = jnp.sum(pr, axis=-1, keepdims=True)
        o2 = jnp.dot(pr.astype(BF16), vc, preferred_element_type=F32) / den
        o = _merge_heads(o2, tm)
        gate = zq_ref[0, :, w_na + p * LANES:w_na + (p + 1) * LANES].astype(F32)
        ybuf[:, lanes] = (o * gate).astype(BF16)
    o_ref[0] = _residual_out(x_ref, gate_ref, zc_ref, ybuf, woutc_ref, wouta_ref)


def _latent_mixer(x, gate, zq, zc, zkv, zkv_c, bias, woutc, wouta, fg, *, tm, final):
    b, n, d = x.shape
    n_ctx = zkv_c.shape[1]
    w_na = d // 2
    n_pairs = w_na // LANES
    halo_kv = (NA_ROWS // 2) * GRID_W
    nt = n // tm
    kv_per = tm // halo_kv

    def const(shape):
        return pl.BlockSpec(shape, lambda bi, i: (0,) * len(shape), pipeline_mode=pl.Buffered(1))

    tile = lambda width: pl.BlockSpec((1, tm, width), lambda bi, i: (bi, i, 0))
    in_specs = [
        tile(d),
        pl.BlockSpec((1, 1, d), lambda bi, i: (bi, 0, 0)),
        tile(d), tile(d // 2), tile(d),
        pl.BlockSpec((1, halo_kv, d), lambda bi, i: (bi, jnp.maximum(i * kv_per - 1, 0), 0)),
        pl.BlockSpec((1, halo_kv, d), lambda bi, i: (bi, jnp.minimum((i + 1) * kv_per, n // halo_kv - 1), 0)),
        pl.BlockSpec((1, n_ctx, d), lambda bi, i: (bi, 0, 0)),
        const(bias.shape), const(woutc.shape), const(wouta.shape), const(fg.shape),
    ]
    return pl.pallas_call(
        functools.partial(_latent_mixer_kernel, tm=tm, d=d, rows_total=n // GRID_W, final=final),
        grid=(b, nt),
        in_specs=in_specs,
        out_specs=tile(d),
        out_shape=jax.ShapeDtypeStruct((b, n, d), F32),
        scratch_shapes=[
            pltpu.VMEM((tm + 2 * halo_kv, w_na + n_pairs * MXU_N), BF16),
            pltpu.VMEM((n_ctx, n_pairs * MXU_N), BF16),
            pltpu.VMEM((tm, w_na), BF16),
        ],
        compiler_params=pltpu.CompilerParams(
            dimension_semantics=("parallel", "parallel"), vmem_limit_bytes=VMEM_LIMIT),
        name="latent_mixer",
    )(x, gate, zq, zc, zkv, zkv, zkv, zkv_c, bias, woutc, wouta, fg)


def _ctx_mixer(x, gate, zq, zc, zkv, woutc, wouta):
    b, n, d = x.shape
    const2 = lambda shape: pl.BlockSpec(shape, lambda bi: (0, 0))
    tile = lambda width: pl.BlockSpec((1, n, width), lambda bi: (bi, 0, 0))
    return pl.pallas_call(
        functools.partial(_ctx_mixer_kernel, tm=n, d=d),
        grid=(b,),
        in_specs=[
            tile(d),
            pl.BlockSpec((1, 1, d), lambda bi: (0, 0, 0)),
            tile(d), tile(d // 2), tile(d),
            const2(woutc.shape), const2(wouta.shape),
        ],
        out_specs=tile(d),
        out_shape=jax.ShapeDtypeStruct((b, n, d), F32),
        scratch_shapes=[pltpu.VMEM((n, d // 2), BF16)],
        compiler_params=pltpu.CompilerParams(
            dimension_semantics=("parallel",), vmem_limit_bytes=VMEM_LIMIT),
        name="ctx_mixer",
    )(x, gate, zq, zc, zkv, woutc, wouta)


def _bias_tables(rpb):
    cols = np.arange(GRID_W)
    col_start = np.clip(cols - NA_COLS // 2, 0, GRID_W - NA_COLS)
    inside = (cols[None, :] >= col_start[:, None]) & (cols[None, :] < col_start[:, None] + NA_COLS)
    rel = cols[None, :] - cols[:, None] + (NA_COLS - 1)
    select = (np.arange(2 * NA_COLS - 1)[:, None, None] == rel[None]) & inside[None]
    tab = jnp.einsum('lhdj,jqk->lhdqk', rpb * LOG2_E, jnp.asarray(select, F32),
                     precision=lax.Precision.HIGHEST)
    tab = tab + jnp.asarray(np.where(inside, 0.0, MASK_BIAS), F32)
    nxt = jnp.concatenate([tab[:, :, 1:], jnp.full_like(tab[:, :, :1], MASK_BIAS)], axis=2)
    return jnp.concatenate([tab, nxt], axis=-1).astype(F32)


def kernel(x, c, ctx, c_ctx, norm_g, w_ada, b_ada, w_in, conv_sc, rpb, conv_cf, conv_cf_b,
           ln_cf_g, ln_cf_b, w_out, final_g):
    depth = w_in.shape[0]
    b, n, d = x.shape
    n_ctx = ctx.shape[1]
    w_sc = d // 4
    mod_rows = 8 * ((b + 1 + 7) // 8)
    cc = jnp.concatenate([c, c_ctx[None, :], jnp.zeros((mod_rows - b - 1, d), F32)], axis=0)
    mods = _modulation(cc, w_ada, b_ada)
    w_in_b = w_in.astype(BF16)
    w_out_b = w_out.astype(BF16)
    w_out_c = jnp.concatenate([w_out_b[:, 0:w_sc], w_out_b[:, 3 * w_sc:4 * w_sc]], axis=1)
    w_out_a = w_out_b[:, w_sc:3 * w_sc]
    bias = _bias_tables(rpb)
    fg = final_g.reshape(1, d)

    for l in range(depth):
        final = l == depth - 1
        shift = mods[l, :b, 0:d].reshape(b, 1, d)
        scale = mods[l, :b, d:2 * d].reshape(b, 1, d)
        gate = mods[l, :b, 2 * d:3 * d].reshape(b, 1, d)
        shift_c = mods[l, b:b + 1, 0:d].reshape(1, 1, d)
        scale_c = mods[l, b:b + 1, d:2 * d].reshape(1, 1, d)
        gate_c = mods[l, b:b + 1, 2 * d:3 * d].reshape(1, 1, d)
        g = norm_g[l].reshape(1, d)
        conv = (conv_sc[l], conv_cf[l]) + tuple(v[l].reshape(1, -1) for v in (conv_cf_b, ln_cf_g, ln_cf_b))

        zq_c, zc_c, zkv_c = _proj(ctx, g, scale_c, shift_c, w_in_b[l], *conv, tm=n_ctx)
        zq, zc, zkv = _proj(x, g, scale, shift, w_in_b[l], *conv, tm=TM_LATENT)
        x = _latent_mixer(x, gate, zq, zc, zkv, zkv_c, bias[l], w_out_c[l], w_out_a[l], fg,
                          tm=TM_LATENT, final=final)
        if not final:
            ctx = _ctx_mixer(ctx, gate_c, zq_c, zc_c, zkv_c, w_out_c[l], w_out_a[l])
    return x
```

```python
import functools

import numpy as np
import jax
import jax.numpy as jnp
from jax import lax
from jax.experimental import pallas as pl
from jax.experimental.pallas import tpu as pltpu

F32 = jnp.float32
BF16 = jnp.bfloat16

GRID_W = 64
HEAD_DIM = 64
NA_ROWS = 8
NA_COLS = 16
SC_K = 3
CF_K = 31
NORM_EPS = 1e-6
LN_EPS = 1e-5
MASK_BIAS = -1e30

LANES = 128
SUBLANES = 8
MXU_N = 256
HALO_U = 16
LOG2_E = 1.4426950408889634
VMEM_LIMIT = 56 * 1024 * 1024
TM_LATENT = 512
OUT_GROUP = 256


def _silu(v):
    return v * (1.0 / (1.0 + jnp.exp(-v)))


def _sigmoid(v):
    return 1.0 / (1.0 + jnp.exp(-v))


def _mod_kernel(c_ref, w_ref, b_ref, o_ref):
    a = _silu(c_ref[...]).astype(BF16)
    o_ref[0] = jnp.dot(a, w_ref[0].astype(BF16), preferred_element_type=F32) + b_ref[0]


def _modulation(cc, w_ada, b_ada):
    depth, d, d3 = w_ada.shape
    rows = cc.shape[0]
    tn = 1024
    return pl.pallas_call(
        _mod_kernel,
        grid=(depth, d3 // tn),
        in_specs=[
            pl.BlockSpec((rows, d), lambda l, n: (0, 0)),
            pl.BlockSpec((1, d, tn), lambda l, n: (l, 0, n)),
            pl.BlockSpec((1, 1, tn), lambda l, n: (l, 0, n)),
        ],
        out_specs=pl.BlockSpec((1, rows, tn), lambda l, n: (l, 0, n)),
        out_shape=jax.ShapeDtypeStruct((depth, rows, d3), F32),
        compiler_params=pltpu.CompilerParams(vmem_limit_bytes=VMEM_LIMIT),
        name="adaln_mod",
    )(cc, w_ada, b_ada.reshape(depth, 1, d3))


class _DepthwiseConv:
    def __init__(self, ubuf, w_ref, lane0, w_lane0, width, ksize):
        self.ubuf, self.w_ref, self.lanes, self.ksize = ubuf, w_ref, slice(lane0, lane0 + width), ksize
        self.w_lanes = slice(w_lane0, w_lane0 + width)
        self.shifts = sorted({(j - ksize // 2) % SUBLANES for j in range(ksize)})
        self.rotated = {}
        self.row = lax.broadcasted_iota(jnp.int32, (SUBLANES, width), 0)

    def _rotated_partial(self, shift, row_tile):
        key = (shift, row_tile)
        if key not in self.rotated:
            acc = None
            for j in range(self.ksize):
                offset = j - self.ksize // 2
                if offset % SUBLANES != shift:
                    continue
                start = HALO_U + row_tile * SUBLANES + offset - shift
                term = self.ubuf[start:start + SUBLANES, self.lanes] * self.w_ref[j:j + 1, self.w_lanes]
                acc = term if acc is None else acc + term
            self.rotated[key] = acc if shift == 0 else pltpu.roll(acc, SUBLANES - shift, 0)
        return self.rotated[key]

    def tile(self, row_tile):
        out = None
        for shift in self.shifts:
            part = self._rotated_partial(shift, row_tile)
            if shift:
                part = jnp.where(self.row < SUBLANES - shift, part, self._rotated_partial(shift, row_tile + 1))
            out = part if out is None else out + part
        self.rotated = {k: v for k, v in self.rotated.items() if k[1] > row_tile}
        return out

    def rows(self, base, rows):
        return jnp.concatenate([self.tile(base // SUBLANES + k) for k in range(rows // SUBLANES)], axis=0)


def _conv_units(ubuf, gbuf, zcbuf, csc_ref, ccf_ref, cfb_ref, lng_ref, lnb_ref, tm, w_sc):
    chunk = 2 * SUBLANES
    conv_sc = [_DepthwiseConv(ubuf, csc_ref, l, l, LANES, SC_K) for l in range(0, w_sc, LANES)]
    conv_cf = [_DepthwiseConv(ubuf, ccf_ref, w_sc + l, l, LANES, CF_K) for l in range(0, w_sc, LANES)]

    def unit(c0):
        rows = slice(c0, c0 + chunk)
        ya = jnp.concatenate([cv.rows(c0, chunk) for cv in conv_sc], axis=-1) * gbuf[rows, 0:w_sc]
        zcbuf[rows, 0:w_sc] = ya.astype(BF16)
        u = jnp.concatenate([cv.rows(c0, chunk) for cv in conv_cf], axis=-1) + cfb_ref[...]
        mu = jnp.mean(u, axis=-1, keepdims=True)
        uc = u - mu
        var = jnp.mean(uc * uc, axis=-1, keepdims=True)
        un = uc * lax.rsqrt(var + LN_EPS) * lng_ref[...] + lnb_ref[...]
        yc = _silu(un) * gbuf[rows, w_sc:2 * w_sc]
        zcbuf[rows, w_sc:2 * w_sc] = yc.astype(BF16)

    return [functools.partial(unit, c0) for c0 in range(0, tm, chunk)]


def _proj_kernel(*refs, d, tm, has_halo):
    if has_halo:
        x_ref, xp_ref, xn_ref = refs[:3]
        refs = refs[3:]
    else:
        x_ref, refs = refs[0], refs[1:]
    (g_ref, scale_ref, shift_ref, w_ref, csc_ref, ccf_ref, cfb_ref, lng_ref, lnb_ref, woutc_ref,
     zq_ref, yc_ref, zkv_ref, hbuf, ubuf, gbuf, zcbuf) = refs
    w_sc = d // 4
    w_na = d // 2
    o_na = 4 * w_sc
    o_cf = o_na + 4 * w_na
    mul = g_ref[...] * (1.0 + scale_ref[0])

    def norm(x):
        ms = jnp.mean(x * x, axis=-1, keepdims=True)
        return (x * lax.rsqrt(ms + NORM_EPS) * mul + shift_ref[0]).astype(BF16)

    half = tm // 2
    for r in (0, half):
        hbuf[HALO_U + r:HALO_U + r + half, :] = norm(x_ref[0, r:r + half, :])
    tile = slice(HALO_U, HALO_U + tm)
    if has_halo:
        hbuf[0:HALO_U, :] = norm(xp_ref[0])
        hbuf[HALO_U + tm:2 * HALO_U + tm, :] = norm(xn_ref[0])
        wide = slice(0, tm + 2 * HALO_U)
        first = pl.program_id(1) == 0
        last = pl.program_id(1) == pl.num_programs(1) - 1
    else:
        wide = tile

    def proj(start, rows=tile):
        return jnp.dot(hbuf[rows, :], w_ref[:, start:start + MXU_N], preferred_element_type=F32)

    u = jnp.concatenate([proj(2 * w_sc, wide) * proj(0, wide),
                         proj(o_cf, wide) * _sigmoid(proj(o_cf + w_sc, wide))], axis=-1)
    zeros = jnp.zeros((HALO_U, 2 * w_sc), F32)
    if has_halo:
        ubuf[0:HALO_U, :] = jnp.where(first, zeros, u[0:HALO_U])
        ubuf[tile, :] = u[HALO_U:HALO_U + tm]
        ubuf[HALO_U + tm:2 * HALO_U + tm, :] = jnp.where(last, zeros, u[HALO_U + tm:2 * HALO_U + tm])
    else:
        ubuf[0:HALO_U, :] = zeros
        ubuf[tile, :] = u
        ubuf[HALO_U + tm:2 * HALO_U + tm, :] = zeros
    gbuf[:, 0:w_sc] = proj(w_sc) * _silu(proj(3 * w_sc))
    gbuf[:, w_sc:2 * w_sc] = _silu(proj(o_cf + 2 * w_sc))

    def store(ref, col, value):
        ref[0, :, col:col + MXU_N] = value.astype(BF16)

    def store_value_slabs(col, value):
        ones = jnp.ones((tm, LANES), BF16)
        for k in range(MXU_N // LANES):
            slab = w_na + (col // LANES + k) * MXU_N
            zkv_ref[0, :, slab:slab + LANES] = value[:, k * LANES:(k + 1) * LANES].astype(BF16)
            zkv_ref[0, :, slab + LANES:slab + MXU_N] = ones

    q_scale = HEAD_DIM ** -0.5 * LOG2_E
    attn_units = []
    for c in range(0, w_na, MXU_N):
        attn_units += [
            lambda c=c: store(zq_ref, c, proj(o_na + c) * q_scale),
            lambda c=c: store(zkv_ref, c, proj(o_na + w_na + c)),
            lambda c=c: store_value_slabs(c, proj(o_na + 2 * w_na + c)),
            lambda c=c: store(zq_ref, w_na + c, _silu(proj(o_na + 3 * w_na + c))),
        ]
    conv_units = _conv_units(ubuf, gbuf, zcbuf, csc_ref, ccf_ref, cfb_ref, lng_ref, lnb_ref, tm, w_sc)
    per = max(len(conv_units) // len(attn_units), 1)
    chunk = tm // len(conv_units)
    group = OUT_GROUP // chunk
    for k, conv_unit in enumerate(conv_units):
        if k % per == 0 and attn_units:
            attn_units.pop(0)()
        conv_unit()
        if (k + 1) % group == 0:
            rows = slice((k + 1 - group) * chunk, (k + 1) * chunk)
            yc_ref[0, rows, :] = jnp.dot(zcbuf[rows, :], woutc_ref[...], preferred_element_type=F32)
    for attn_unit in attn_units:
        attn_unit()


def _proj(x, g, scale, shift, w, csc, ccf, cfb, lng, lnb, woutc, *, tm):
    b, n, d = x.shape
    d_proj = w.shape[1]
    w_sc = d // 4
    nt = n // tm
    has_halo = nt > 1
    kv_width = d // 2 + (d // 2 // LANES) * MXU_N
    sb = 0 if scale.shape[0] == 1 else 1
    per = tm // HALO_U

    def const(shape):
        return pl.BlockSpec(shape, lambda bi, i: (0,) * len(shape), pipeline_mode=pl.Buffered(1))

    vec = pl.BlockSpec((1, 1, d), lambda bi, i: (bi * sb, 0, 0))
    out = lambda width: pl.BlockSpec((1, tm, width), lambda bi, i: (bi, i, 0))
    x_specs = [pl.BlockSpec((1, tm, d), lambda bi, i: (bi, i, 0))]
    x_args = [x]
    if has_halo:
        x_specs += [
            pl.BlockSpec((1, HALO_U, d), lambda bi, i: (bi, jnp.maximum(i * per - 1, 0), 0)),
            pl.BlockSpec((1, HALO_U, d), lambda bi, i: (bi, jnp.minimum((i + 1) * per, n // HALO_U - 1), 0)),
        ]
        x_args += [x, x]
    return pl.pallas_call(
        functools.partial(_proj_kernel, d=d, tm=tm, has_halo=has_halo),
        grid=(b, nt),
        in_specs=x_specs + [
            const((1, d)), vec, vec, const((d, d_proj)),
            const(csc.shape), const(ccf.shape), const(cfb.shape), const(lng.shape), const(lnb.shape),
            const(woutc.shape),
        ],
        out_specs=[out(d), out(d), out(kv_width)],
        out_shape=[
            jax.ShapeDtypeStruct((b, n, d), BF16),
            jax.ShapeDtypeStruct((b, n, d), F32),
            jax.ShapeDtypeStruct((b, n, kv_width), BF16),
        ],
        scratch_shapes=[
            pltpu.VMEM((tm + 2 * HALO_U, d), BF16),
            pltpu.VMEM((tm + 2 * HALO_U, 2 * w_sc), F32),
            pltpu.VMEM((tm, 2 * w_sc), F32),
            pltpu.VMEM((tm, 2 * w_sc), BF16),
        ],
        compiler_params=pltpu.CompilerParams(
            dimension_semantics=("parallel", "parallel"), vmem_limit_bytes=VMEM_LIMIT),
        name="proj_conv",
    )(*x_args, g, scale, shift, w, csc, ccf, cfb, lng, lnb, woutc)


def _split_heads(q2):
    lane = lax.broadcasted_iota(jnp.int32, q2.shape, 1)
    zero = jnp.zeros_like(q2)
    return jnp.concatenate([jnp.where(lane < HEAD_DIM, q2, zero),
                            jnp.where(lane >= HEAD_DIM, q2, zero)], axis=0)


def _merge_heads(o2, rows):
    lane = lax.broadcasted_iota(jnp.int32, (rows, LANES), 1)
    return jnp.where(lane < HEAD_DIM, o2[0:rows], o2[rows:2 * rows])


def _lane_tiles(a):
    return [a[:, i:i + LANES] for i in range(0, a.shape[1], LANES)]


_NT = (((1,), (1,)), ((), ()))


def _residual_out(x_ref, gate_ref, yc_ref, ybuf, wouta_ref):
    y = yc_ref[0] + jnp.dot(ybuf[...], wouta_ref[...], preferred_element_type=F32)
    return x_ref[0] + gate_ref[0] * y


def _latent_mixer_kernel(x_ref, gate_ref, zq_ref, yc_ref, kv_ref, kvc_ref, bias_ref, wouta_ref, fg_ref,
                         o_ref, ybuf, *, tm, d, rows_total, final):
    w_na = d // 2
    n_pairs = w_na // LANES
    tile_rows = tm // GRID_W
    kv_rows = kv_ref.shape[1] // GRID_W
    r0 = pl.program_id(1) * tile_rows
    kv_row0 = jnp.clip(r0 - NA_ROWS // 2, 0, rows_total - kv_rows)
    n_keys = NA_ROWS * GRID_W

    row_params = []
    for j in range(tile_rows):
        r = r0 + j
        rs = jnp.clip(r - NA_ROWS // 2, 0, rows_total - NA_ROWS)
        kstart = pl.multiple_of((rs - kv_row0) * GRID_W, GRID_W)
        row_params.append((kstart, rs - r + (NA_ROWS - 1)))

    def scores(it):
        j, p = divmod(it, n_pairs)
        kstart, bstart = row_params[j]
        lanes = slice(p * LANES, (p + 1) * LANES)
        qs = _split_heads(zq_ref[0, j * GRID_W:(j + 1) * GRID_W, lanes])
        kw = kv_ref[0, pl.ds(kstart, n_keys), lanes]
        kc = kvc_ref[0, :, lanes]
        s_w = lax.dot_general(qs, kw, _NT, preferred_element_type=F32)
        s_c = lax.dot_general(qs, kc, _NT, preferred_element_type=F32)
        bias = jnp.concatenate(
            [jnp.concatenate([bias_ref[2 * p, bstart + 2 * u], bias_ref[2 * p + 1, bstart + 2 * u]], axis=0)
             for u in range(NA_ROWS // 2)], axis=-1)
        s_w = s_w + bias
        m = jnp.max(functools.reduce(jnp.maximum, _lane_tiles(s_w) + _lane_tiles(s_c)), axis=-1, keepdims=True)
        return s_w, s_c, m

    def probs(s_w, s_c, m):
        return jnp.exp2(s_w - m).astype(BF16), jnp.exp2(s_c - m).astype(BF16)

    def outputs(it, p_w, p_c):
        j, p = divmod(it, n_pairs)
        kstart = row_params[j][0]
        slab = slice(w_na + p * MXU_N, w_na + (p + 1) * MXU_N)
        vw = kv_ref[0, pl.ds(kstart, n_keys), slab]
        vc = kvc_ref[0, :, slab]
        oa = jnp.dot(p_w, vw, preferred_element_type=F32) + jnp.dot(p_c, vc, preferred_element_type=F32)
        o = _merge_heads(oa[:, 0:LANES] / oa[:, LANES:MXU_N], GRID_W)
        rows = slice(j * GRID_W, (j + 1) * GRID_W)
        gate = zq_ref[0, rows, w_na + p * LANES:w_na + (p + 1) * LANES].astype(F32)
        ybuf[rows, p * LANES:(p + 1) * LANES] = (o * gate).astype(BF16)

    n_items = tile_rows * n_pairs
    st_scores, st_probs = {}, {}
    for k in range(n_items + 2):
        if k < n_items:
            st_scores[k] = scores(k)
        if 0 <= k - 1 < n_items:
            st_probs[k - 1] = probs(*st_scores.pop(k - 1))
        if 0 <= k - 2 < n_items:
            outputs(k - 2, *st_probs.pop(k - 2))

    xn = _residual_out(x_ref, gate_ref, yc_ref, ybuf, wouta_ref)
    if final:
        ms = jnp.mean(xn * xn, axis=-1, keepdims=True)
        xn = xn * lax.rsqrt(ms + NORM_EPS) * fg_ref[...]
    o_ref[0] = xn


def _ctx_mixer_kernel(x_ref, gate_ref, zq_ref, yc_ref, zkv_ref, wouta_ref,
                      o_ref, ybuf, *, tm, d):
    w_na = d // 2
    for p in range(w_na // LANES):
        lanes = slice(p * LANES, (p + 1) * LANES)
        qs = _split_heads(zq_ref[0, :, lanes])
        kc = zkv_ref[0, :, lanes]
        vc = zkv_ref[0, :, w_na + p * MXU_N:w_na + p * MXU_N + LANES]
        s = lax.dot_general(qs, kc, _NT, preferred_element_type=F32)
        m = jnp.max(s, axis=-1, keepdims=True)
        pr = jnp.exp2(s - m)
        den = jnp.sum(pr, axis=-1, keepdims=True)
        o2 = jnp.dot(pr.astype(BF16), vc, preferred_element_type=F32) / den
        o = _merge_heads(o2, tm)
        gate = zq_ref[0, :, w_na + p * LANES:w_na + (p + 1) * LANES].astype(F32)
        ybuf[:, lanes] = (o * gate).astype(BF16)
    o_ref[0] = _residual_out(x_ref, gate_ref, yc_ref, ybuf, wouta_ref)


def _latent_mixer(x, gate, zq, yc, zkv, zkv_c, bias, wouta, fg, *, tm, final):
    b, n, d = x.shape
    n_ctx, kv_width = zkv_c.shape[1], zkv.shape[2]
    nt = n // tm
    kv_window = tm + NA_ROWS * GRID_W
    halo = (NA_ROWS // 2) * GRID_W

    def const(shape):
        return pl.BlockSpec(shape, lambda bi, i: (0,) * len(shape), pipeline_mode=pl.Buffered(1))

    tile = lambda width: pl.BlockSpec((1, tm, width), lambda bi, i: (bi, i, 0))
    in_specs = [
        tile(d),
        pl.BlockSpec((1, 1, d), lambda bi, i: (bi, 0, 0)),
        tile(d), tile(d),
        pl.BlockSpec((pl.Element(1), pl.Element(kv_window), pl.Element(kv_width)),
                     lambda bi, i: (bi, halo * jnp.clip(i * (tm // halo) - 1, 0, (n - kv_window) // halo), 0)),
        pl.BlockSpec((1, n_ctx, kv_width), lambda bi, i: (bi, 0, 0)),
        const(bias.shape), const(wouta.shape), const(fg.shape),
    ]
    return pl.pallas_call(
        functools.partial(_latent_mixer_kernel, tm=tm, d=d, rows_total=n // GRID_W, final=final),
        grid=(b, nt),
        in_specs=in_specs,
        out_specs=tile(d),
        out_shape=jax.ShapeDtypeStruct((b, n, d), F32),
        scratch_shapes=[pltpu.VMEM((tm, d // 2), BF16)],
        compiler_params=pltpu.CompilerParams(
            dimension_semantics=("parallel", "parallel"), vmem_limit_bytes=VMEM_LIMIT),
        name="latent_mixer",
    )(x, gate, zq, yc, zkv, zkv_c, bias, wouta, fg)


def _ctx_mixer(x, gate, zq, yc, zkv, wouta):
    b, n, d = x.shape
    const2 = lambda shape: pl.BlockSpec(shape, lambda bi: (0, 0))
    tile = lambda width: pl.BlockSpec((1, n, width), lambda bi: (bi, 0, 0))
    return pl.pallas_call(
        functools.partial(_ctx_mixer_kernel, tm=n, d=d),
        grid=(b,),
        in_specs=[
            tile(d),
            pl.BlockSpec((1, 1, d), lambda bi: (0, 0, 0)),
            tile(d), tile(d), tile(zkv.shape[2]),
            const2(wouta.shape),
        ],
        out_specs=tile(d),
        out_shape=jax.ShapeDtypeStruct((b, n, d), F32),
        scratch_shapes=[pltpu.VMEM((n, d // 2), BF16)],
        compiler_params=pltpu.CompilerParams(
            dimension_semantics=("parallel",), vmem_limit_bytes=VMEM_LIMIT),
        name="ctx_mixer",
    )(x, gate, zq, yc, zkv, wouta)


def _bias_tables(rpb):
    cols = np.arange(GRID_W)
    col_start = np.clip(cols - NA_COLS // 2, 0, GRID_W - NA_COLS)
    inside = (cols[None, :] >= col_start[:, None]) & (cols[None, :] < col_start[:, None] + NA_COLS)
    rel = cols[None, :] - cols[:, None] + (NA_COLS - 1)
    select = (np.arange(2 * NA_COLS - 1)[:, None, None] == rel[None]) & inside[None]
    tab = jnp.einsum('lhdj,jqk->lhdqk', rpb * LOG2_E, jnp.asarray(select, F32),
                     precision=lax.Precision.HIGHEST)
    tab = tab + jnp.asarray(np.where(inside, 0.0, MASK_BIAS), F32)
    nxt = jnp.concatenate([tab[:, :, 1:], jnp.full_like(tab[:, :, :1], MASK_BIAS)], axis=2)
    return jnp.concatenate([tab, nxt], axis=-1).astype(F32)


def kernel(x, c, ctx, c_ctx, norm_g, w_ada, b_ada, w_in, conv_sc, rpb, conv_cf, conv_cf_b,
           ln_cf_g, ln_cf_b, w_out, final_g):
    depth = w_in.shape[0]
    b, n, d = x.shape
    n_ctx = ctx.shape[1]
    w_sc = d // 4
    mod_rows = 8 * ((b + 1 + 7) // 8)
    cc = jnp.concatenate([c, c_ctx[None, :], jnp.zeros((mod_rows - b - 1, d), F32)], axis=0)
    mods = _modulation(cc, w_ada, b_ada)
    w_in_b = w_in.astype(BF16)
    w_out_b = w_out.astype(BF16)
    w_out_c = jnp.concatenate([w_out_b[:, 0:w_sc], w_out_b[:, 3 * w_sc:4 * w_sc]], axis=1)
    w_out_a = w_out_b[:, w_sc:3 * w_sc]
    bias = _bias_tables(rpb)
    fg = final_g.reshape(1, d)

    for l in range(depth):
        final = l == depth - 1
        shift = mods[l, :b, 0:d].reshape(b, 1, d)
        scale = mods[l, :b, d:2 * d].reshape(b, 1, d)
        gate = mods[l, :b, 2 * d:3 * d].reshape(b, 1, d)
        shift_c = mods[l, b:b + 1, 0:d].reshape(1, 1, d)
        scale_c = mods[l, b:b + 1, d:2 * d].reshape(1, 1, d)
        gate_c = mods[l, b:b + 1, 2 * d:3 * d].reshape(1, 1, d)
        g = norm_g[l].reshape(1, d)
        conv = (conv_sc[l], conv_cf[l]) + tuple(v[l].reshape(1, -1) for v in (conv_cf_b, ln_cf_g, ln_cf_b))

        zq_c, yc_c, zkv_c = _proj(ctx, g, scale_c, shift_c, w_in_b[l], *conv, w_out_c[l], tm=n_ctx)
        zq, yc, zkv = _proj(x, g, scale, shift, w_in_b[l], *conv, w_out_c[l], tm=TM_LATENT)
        x = _latent_mixer(x, gate, zq, yc, zkv, zkv_c, bias[l], w_out_a[l], fg, tm=TM_LATENT, final=final)
        if not final:
            ctx = _ctx_mixer(ctx, gate_c, zq_c, yc_c, zkv_c, w_out_a[l])
    return x
```

```python
import functools

import numpy as np
import jax
import jax.numpy as jnp
from jax import lax
from jax.experimental import pallas as pl
from jax.experimental.pallas import tpu as pltpu

F32 = jnp.float32
BF16 = jnp.bfloat16

GRID_W = 64
HEAD_DIM = 64
NA_ROWS = 8
NA_COLS = 16
SC_K = 3
CF_K = 31
NORM_EPS = 1e-6
LN_EPS = 1e-5
MASK_BIAS = -1e30

LANES = 128
SUBLANES = 8
MXU_N = 256
HALO_U = 16
LOG2_E = 1.4426950408889634
VMEM_LIMIT = 56 * 1024 * 1024
TM_LATENT = 512
STAGE_SKEW = 3
OUT_GROUP = 256


def _silu(v):
    return v * (1.0 / (1.0 + jnp.exp(-v)))


def _sigmoid(v):
    return 1.0 / (1.0 + jnp.exp(-v))


def _mod_kernel(c_ref, w_ref, b_ref, o_ref):
    a = _silu(c_ref[...]).astype(BF16)
    o_ref[0] = jnp.dot(a, w_ref[0].astype(BF16), preferred_element_type=F32) + b_ref[0]


def _modulation(cc, w_ada, b_ada):
    depth, d, d3 = w_ada.shape
    rows = cc.shape[0]
    tn = 1024
    return pl.pallas_call(
        _mod_kernel,
        grid=(depth, d3 // tn),
        in_specs=[
            pl.BlockSpec((rows, d), lambda l, n: (0, 0)),
            pl.BlockSpec((1, d, tn), lambda l, n: (l, 0, n)),
            pl.BlockSpec((1, 1, tn), lambda l, n: (l, 0, n)),
        ],
        out_specs=pl.BlockSpec((1, rows, tn), lambda l, n: (l, 0, n)),
        out_shape=jax.ShapeDtypeStruct((depth, rows, d3), F32),
        compiler_params=pltpu.CompilerParams(vmem_limit_bytes=VMEM_LIMIT),
        name="adaln_mod",
    )(cc, w_ada, b_ada.reshape(depth, 1, d3))


class _DepthwiseConv:
    def __init__(self, ubuf, w_ref, lane0, w_lane0, width, ksize):
        self.ubuf, self.w_ref, self.lanes, self.ksize = ubuf, w_ref, slice(lane0, lane0 + width), ksize
        self.w_lanes = slice(w_lane0, w_lane0 + width)
        self.shifts = sorted({(j - ksize // 2) % SUBLANES for j in range(ksize)})
        self.rotated = {}
        self.row = lax.broadcasted_iota(jnp.int32, (SUBLANES, width), 0)

    def _rotated_partial(self, shift, row_tile):
        key = (shift, row_tile)
        if key not in self.rotated:
            acc = None
            for j in range(self.ksize):
                offset = j - self.ksize // 2
                if offset % SUBLANES != shift:
                    continue
                start = HALO_U + row_tile * SUBLANES + offset - shift
                term = self.ubuf[start:start + SUBLANES, self.lanes] * self.w_ref[j:j + 1, self.w_lanes]
                acc = term if acc is None else acc + term
            self.rotated[key] = acc if shift == 0 else pltpu.roll(acc, SUBLANES - shift, 0)
        return self.rotated[key]

    def tile(self, row_tile):
        out = None
        for shift in self.shifts:
            part = self._rotated_partial(shift, row_tile)
            if shift:
                part = jnp.where(self.row < SUBLANES - shift, part, self._rotated_partial(shift, row_tile + 1))
            out = part if out is None else out + part
        self.rotated = {k: v for k, v in self.rotated.items() if k[1] > row_tile}
        return out

    def rows(self, base, rows):
        return jnp.concatenate([self.tile(base // SUBLANES + k) for k in range(rows // SUBLANES)], axis=0)


def _conv_units(ubuf, gbuf, zcbuf, csc_ref, ccf_ref, cfb_ref, lng_ref, lnb_ref, tm, w_sc):
    chunk = 2 * SUBLANES
    conv_sc = [_DepthwiseConv(ubuf, csc_ref, l, l, LANES, SC_K) for l in range(0, w_sc, LANES)]
    conv_cf = [_DepthwiseConv(ubuf, ccf_ref, w_sc + l, l, LANES, CF_K) for l in range(0, w_sc, LANES)]

    def unit(c0):
        rows = slice(c0, c0 + chunk)
        ya = jnp.concatenate([cv.rows(c0, chunk) for cv in conv_sc], axis=-1) * gbuf[rows, 0:w_sc]
        zcbuf[rows, 0:w_sc] = ya.astype(BF16)
        u = jnp.concatenate([cv.rows(c0, chunk) for cv in conv_cf], axis=-1) + cfb_ref[...]
        mu = jnp.mean(u, axis=-1, keepdims=True)
        uc = u - mu
        var = jnp.mean(uc * uc, axis=-1, keepdims=True)
        un = uc * lax.rsqrt(var + LN_EPS) * lng_ref[...] + lnb_ref[...]
        yc = _silu(un) * gbuf[rows, w_sc:2 * w_sc]
        zcbuf[rows, w_sc:2 * w_sc] = yc.astype(BF16)

    return [functools.partial(unit, c0) for c0 in range(0, tm, chunk)]


def _proj_kernel(*refs, d, tm, has_halo):
    if has_halo:
        x_ref, xp_ref, xn_ref = refs[:3]
        refs = refs[3:]
    else:
        x_ref, refs = refs[0], refs[1:]
    (g_ref, scale_ref, shift_ref, w_ref, csc_ref, ccf_ref, cfb_ref, lng_ref, lnb_ref, woutc_ref,
     zq_ref, yc_ref, zkv_ref, hbuf, ubuf, gbuf, zcbuf) = refs
    w_sc = d // 4
    w_na = d // 2
    o_na = 4 * w_sc
    o_cf = o_na + 4 * w_na
    mul = g_ref[...] * (1.0 + scale_ref[0])

    def norm(x):
        ms = jnp.mean(x * x, axis=-1, keepdims=True)
        return (x * lax.rsqrt(ms + NORM_EPS) * mul + shift_ref[0]).astype(BF16)

    half = tm // 2
    for r in (0, half):
        hbuf[HALO_U + r:HALO_U + r + half, :] = norm(x_ref[0, r:r + half, :])
    tile = slice(HALO_U, HALO_U + tm)
    if has_halo:
        hbuf[0:HALO_U, :] = norm(xp_ref[0])
        hbuf[HALO_U + tm:2 * HALO_U + tm, :] = norm(xn_ref[0])
        wide = slice(0, tm + 2 * HALO_U)
        first = pl.program_id(1) == 0
        last = pl.program_id(1) == pl.num_programs(1) - 1
    else:
        wide = tile

    def proj(start, rows=tile):
        return jnp.dot(hbuf[rows, :], w_ref[:, start:start + MXU_N], preferred_element_type=F32)

    u = jnp.concatenate([proj(2 * w_sc, wide) * proj(0, wide),
                         proj(o_cf, wide) * _sigmoid(proj(o_cf + w_sc, wide))], axis=-1)
    zeros = jnp.zeros((HALO_U, 2 * w_sc), F32)
    if has_halo:
        ubuf[0:HALO_U, :] = jnp.where(first, zeros, u[0:HALO_U])
        ubuf[tile, :] = u[HALO_U:HALO_U + tm]
        ubuf[HALO_U + tm:2 * HALO_U + tm, :] = jnp.where(last, zeros, u[HALO_U + tm:2 * HALO_U + tm])
    else:
        ubuf[0:HALO_U, :] = zeros
        ubuf[tile, :] = u
        ubuf[HALO_U + tm:2 * HALO_U + tm, :] = zeros
    gbuf[:, 0:w_sc] = proj(w_sc) * _silu(proj(3 * w_sc))
    gbuf[:, w_sc:2 * w_sc] = _silu(proj(o_cf + 2 * w_sc))

    def store(ref, col, value):
        ref[0, :, col:col + MXU_N] = value.astype(BF16)

    def store_value_slabs(col, value):
        ones = jnp.ones((tm, LANES), BF16)
        for k in range(MXU_N // LANES):
            slab = w_na + (col // LANES + k) * MXU_N
            zkv_ref[0, :, slab:slab + LANES] = value[:, k * LANES:(k + 1) * LANES].astype(BF16)
            zkv_ref[0, :, slab + LANES:slab + MXU_N] = ones

    q_scale = HEAD_DIM ** -0.5 * LOG2_E
    attn_units = []
    for c in range(0, w_na, MXU_N):
        attn_units += [
            lambda c=c: store(zq_ref, c, proj(o_na + c) * q_scale),
            lambda c=c: store(zkv_ref, c, proj(o_na + w_na + c)),
            lambda c=c: store_value_slabs(c, proj(o_na + 2 * w_na + c)),
            lambda c=c: store(zq_ref, w_na + c, _silu(proj(o_na + 3 * w_na + c))),
        ]
    conv_units = _conv_units(ubuf, gbuf, zcbuf, csc_ref, ccf_ref, cfb_ref, lng_ref, lnb_ref, tm, w_sc)
    per = max(len(conv_units) // len(attn_units), 1)
    chunk = tm // len(conv_units)
    group = OUT_GROUP // chunk
    for k, conv_unit in enumerate(conv_units):
        if k % per == 0 and attn_units:
            attn_units.pop(0)()
        conv_unit()
        if (k + 1) % group == 0:
            rows = slice((k + 1 - group) * chunk, (k + 1) * chunk)
            yc_ref[0, rows, :] = jnp.dot(zcbuf[rows, :], woutc_ref[...], preferred_element_type=F32)
    for attn_unit in attn_units:
        attn_unit()


def _proj(x, g, scale, shift, w, csc, ccf, cfb, lng, lnb, woutc, *, tm):
    b, n, d = x.shape
    d_proj = w.shape[1]
    w_sc = d // 4
    nt = n // tm
    has_halo = nt > 1
    kv_width = d // 2 + (d // 2 // LANES) * MXU_N
    sb = 0 if scale.shape[0] == 1 else 1
    per = tm // HALO_U

    def const(shape):
        return pl.BlockSpec(shape, lambda bi, i: (0,) * len(shape), pipeline_mode=pl.Buffered(1))

    vec = pl.BlockSpec((1, 1, d), lambda bi, i: (bi * sb, 0, 0))
    out = lambda width: pl.BlockSpec((1, tm, width), lambda bi, i: (bi, i, 0))
    x_specs = [pl.BlockSpec((1, tm, d), lambda bi, i: (bi, i, 0))]
    x_args = [x]
    if has_halo:
        x_specs += [
            pl.BlockSpec((1, HALO_U, d), lambda bi, i: (bi, jnp.maximum(i * per - 1, 0), 0)),
            pl.BlockSpec((1, HALO_U, d), lambda bi, i: (bi, jnp.minimum((i + 1) * per, n // HALO_U - 1), 0)),
        ]
        x_args += [x, x]
    return pl.pallas_call(
        functools.partial(_proj_kernel, d=d, tm=tm, has_halo=has_halo),
        grid=(b, nt),
        in_specs=x_specs + [
            const((1, d)), vec, vec, const((d, d_proj)),
            const(csc.shape), const(ccf.shape), const(cfb.shape), const(lng.shape), const(lnb.shape),
            const(woutc.shape),
        ],
        out_specs=[out(d), out(d), out(kv_width)],
        out_shape=[
            jax.ShapeDtypeStruct((b, n, d), BF16),
            jax.ShapeDtypeStruct((b, n, d), F32),
            jax.ShapeDtypeStruct((b, n, kv_width), BF16),
        ],
        scratch_shapes=[
            pltpu.VMEM((tm + 2 * HALO_U, d), BF16),
            pltpu.VMEM((tm + 2 * HALO_U, 2 * w_sc), F32),
            pltpu.VMEM((tm, 2 * w_sc), F32),
            pltpu.VMEM((tm, 2 * w_sc), BF16),
        ],
        compiler_params=pltpu.CompilerParams(
            dimension_semantics=("parallel", "parallel"), vmem_limit_bytes=VMEM_LIMIT),
        name="proj_conv",
    )(*x_args, g, scale, shift, w, csc, ccf, cfb, lng, lnb, woutc)


def _split_heads(q2):
    lane = lax.broadcasted_iota(jnp.int32, q2.shape, 1)
    zero = jnp.zeros_like(q2)
    return jnp.concatenate([jnp.where(lane < HEAD_DIM, q2, zero),
                            jnp.where(lane >= HEAD_DIM, q2, zero)], axis=0)


def _merge_heads(o2, rows):
    lane = lax.broadcasted_iota(jnp.int32, (rows, LANES), 1)
    return jnp.where(lane < HEAD_DIM, o2[0:rows], o2[rows:2 * rows])


def _lane_tiles(a):
    return [a[:, i:i + LANES] for i in range(0, a.shape[1], LANES)]


_NT = (((1,), (1,)), ((), ()))


def _residual_out(x_ref, gate_ref, yc_ref, ybuf, wouta_ref):
    y = yc_ref[0] + jnp.dot(ybuf[...], wouta_ref[...], preferred_element_type=F32)
    return x_ref[0] + gate_ref[0] * y


def _latent_mixer_kernel(x_ref, gate_ref, zq_ref, yc_ref, kv_ref, kvc_ref, bias_ref, wouta_ref, fg_ref,
                         o_ref, ybuf, *, tm, d, rows_total, final):
    w_na = d // 2
    n_pairs = w_na // LANES
    tile_rows = tm // GRID_W
    kv_rows = kv_ref.shape[1] // GRID_W
    r0 = pl.program_id(1) * tile_rows
    kv_row0 = jnp.clip(r0 - NA_ROWS // 2, 0, rows_total - kv_rows)
    n_keys = NA_ROWS * GRID_W

    row_params = []
    for j in range(tile_rows):
        r = r0 + j
        rs = jnp.clip(r - NA_ROWS // 2, 0, rows_total - NA_ROWS)
        kstart = pl.multiple_of((rs - kv_row0) * GRID_W, GRID_W)
        row_params.append((kstart, rs - r + (NA_ROWS - 1)))

    def scores(it):
        j, p = divmod(it, n_pairs)
        kstart, bstart = row_params[j]
        lanes = slice(p * LANES, (p + 1) * LANES)
        qs = _split_heads(zq_ref[0, j * GRID_W:(j + 1) * GRID_W, lanes])
        kw = kv_ref[0, pl.ds(kstart, n_keys), lanes]
        kc = kvc_ref[0, :, lanes]
        s_w = lax.dot_general(qs, kw, _NT, preferred_element_type=F32)
        s_c = lax.dot_general(qs, kc, _NT, preferred_element_type=F32)
        bias = jnp.concatenate(
            [jnp.concatenate([bias_ref[2 * p, bstart + 2 * u], bias_ref[2 * p + 1, bstart + 2 * u]], axis=0)
             for u in range(NA_ROWS // 2)], axis=-1)
        s_w = s_w + bias
        m = jnp.max(functools.reduce(jnp.maximum, _lane_tiles(s_w) + _lane_tiles(s_c)), axis=-1, keepdims=True)
        return s_w, s_c, m

    def probs(s_w, s_c, m):
        return jnp.exp2(s_w - m).astype(BF16), jnp.exp2(s_c - m).astype(BF16)

    def outputs(it, p_w, p_c):
        j, p = divmod(it, n_pairs)
        kstart = row_params[j][0]
        slab = slice(w_na + p * MXU_N, w_na + (p + 1) * MXU_N)
        vw = kv_ref[0, pl.ds(kstart, n_keys), slab]
        vc = kvc_ref[0, :, slab]
        oa = jnp.dot(p_w, vw, preferred_element_type=F32) + jnp.dot(p_c, vc, preferred_element_type=F32)
        o = _merge_heads(oa[:, 0:LANES] / oa[:, LANES:MXU_N], GRID_W)
        rows = slice(j * GRID_W, (j + 1) * GRID_W)
        gate = zq_ref[0, rows, w_na + p * LANES:w_na + (p + 1) * LANES].astype(F32)
        ybuf[rows, p * LANES:(p + 1) * LANES] = (o * gate).astype(BF16)

    n_items = tile_rows * n_pairs
    st_scores, st_probs = {}, {}
    for k in range(n_items + 2 * STAGE_SKEW):
        if k < n_items:
            st_scores[k] = scores(k)
        if 0 <= k - STAGE_SKEW < n_items:
            st_probs[k - STAGE_SKEW] = probs(*st_scores.pop(k - STAGE_SKEW))
        if 0 <= k - 2 * STAGE_SKEW < n_items:
            outputs(k - 2 * STAGE_SKEW, *st_probs.pop(k - 2 * STAGE_SKEW))

    xn = _residual_out(x_ref, gate_ref, yc_ref, ybuf, wouta_ref)
    if final:
        ms = jnp.mean(xn * xn, axis=-1, keepdims=True)
        xn = xn * lax.rsqrt(ms + NORM_EPS) * fg_ref[...]
    o_ref[0] = xn


def _ctx_mixer_kernel(x_ref, gate_ref, zq_ref, yc_ref, zkv_ref, wouta_ref,
                      o_ref, ybuf, *, tm, d):
    w_na = d // 2
    for p in range(w_na // LANES):
        lanes = slice(p * LANES, (p + 1) * LANES)
        qs = _split_heads(zq_ref[0, :, lanes])
        kc = zkv_ref[0, :, lanes]
        vc = zkv_ref[0, :, w_na + p * MXU_N:w_na + p * MXU_N + LANES]
        s = lax.dot_general(qs, kc, _NT, preferred_element_type=F32)
        m = jnp.max(s, axis=-1, keepdims=True)
        pr = jnp.exp2(s - m)
        den = jnp.sum(pr, axis=-1, keepdims=True)
        o2 = jnp.dot(pr.astype(BF16), vc, preferred_element_type=F32) / den
        o = _merge_heads(o2, tm)
        gate = zq_ref[0, :, w_na + p * LANES:w_na + (p + 1) * LANES].astype(F32)
        ybuf[:, lanes] = (o * gate).astype(BF16)
    o_ref[0] = _residual_out(x_ref, gate_ref, yc_ref, ybuf, wouta_ref)


def _latent_mixer(x, gate, zq, yc, zkv, zkv_c, bias, wouta, fg, *, tm, final):
    b, n, d = x.shape
    n_ctx, kv_width = zkv_c.shape[1], zkv.shape[2]
    nt = n // tm
    kv_window = tm + NA_ROWS * GRID_W
    halo = (NA_ROWS // 2) * GRID_W

    def const(shape):
        return pl.BlockSpec(shape, lambda bi, i: (0,) * len(shape), pipeline_mode=pl.Buffered(1))

    tile = lambda width: pl.BlockSpec((1, tm, width), lambda bi, i: (bi, i, 0))
    in_specs = [
        tile(d),
        pl.BlockSpec((1, 1, d), lambda bi, i: (bi, 0, 0)),
        tile(d), tile(d),
        pl.BlockSpec((pl.Element(1), pl.Element(kv_window), pl.Element(kv_width)),
                     lambda bi, i: (bi, halo * jnp.clip(i * (tm // halo) - 1, 0, (n - kv_window) // halo), 0)),
        pl.BlockSpec((1, n_ctx, kv_width), lambda bi, i: (bi, 0, 0)),
        const(bias.shape), const(wouta.shape), const(fg.shape),
    ]
    return pl.pallas_call(
        functools.partial(_latent_mixer_kernel, tm=tm, d=d, rows_total=n // GRID_W, final=final),
        grid=(b, nt),
        in_specs=in_specs,
        out_specs=tile(d),
        out_shape=jax.ShapeDtypeStruct((b, n, d), F32),
        scratch_shapes=[pltpu.VMEM((tm, d // 2), BF16)],
        compiler_params=pltpu.CompilerParams(
            dimension_semantics=("parallel", "parallel"), vmem_limit_bytes=VMEM_LIMIT),
        name="latent_mixer",
    )(x, gate, zq, yc, zkv, zkv_c, bias, wouta, fg)


def _ctx_mixer(x, gate, zq, yc, zkv, wouta):
    b, n, d = x.shape
    const2 = lambda shape: pl.BlockSpec(shape, lambda bi: (0, 0))
    tile = lambda width: pl.BlockSpec((1, n, width), lambda bi: (bi, 0, 0))
    return pl.pallas_call(
        functools.partial(_ctx_mixer_kernel, tm=n, d=d),
        grid=(b,),
        in_specs=[
            tile(d),
            pl.BlockSpec((1, 1, d), lambda bi: (0, 0, 0)),
            tile(d), tile(d), tile(zkv.shape[2]),
            const2(wouta.shape),
        ],
        out_specs=tile(d),
        out_shape=jax.ShapeDtypeStruct((b, n, d), F32),
        scratch_shapes=[pltpu.VMEM((n, d // 2), BF16)],
        compiler_params=pltpu.CompilerParams(
            dimension_semantics=("parallel",), vmem_limit_bytes=VMEM_LIMIT),
        name="ctx_mixer",
    )(x, gate, zq, yc, zkv, wouta)


def _bias_tables(rpb):
    cols = np.arange(GRID_W)
    col_start = np.clip(cols - NA_COLS // 2, 0, GRID_W - NA_COLS)
    inside = (cols[None, :] >= col_start[:, None]) & (cols[None, :] < col_start[:, None] + NA_COLS)
    rel = cols[None, :] - cols[:, None] + (NA_COLS - 1)
    select = (np.arange(2 * NA_COLS - 1)[:, None, None] == rel[None]) & inside[None]
    tab = jnp.einsum('lhdj,jqk->lhdqk', rpb * LOG2_E, jnp.asarray(select, F32),
                     precision=lax.Precision.HIGHEST)
    tab = tab + jnp.asarray(np.where(inside, 0.0, MASK_BIAS), F32)
    nxt = jnp.concatenate([tab[:, :, 1:], jnp.full_like(tab[:, :, :1], MASK_BIAS)], axis=2)
    return jnp.concatenate([tab, nxt], axis=-1).astype(F32)


def kernel(x, c, ctx, c_ctx, norm_g, w_ada, b_ada, w_in, conv_sc, rpb, conv_cf, conv_cf_b,
           ln_cf_g, ln_cf_b, w_out, final_g):
    depth = w_in.shape[0]
    b, n, d = x.shape
    n_ctx = ctx.shape[1]
    w_sc = d // 4
    mod_rows = 8 * ((b + 1 + 7) // 8)
    cc = jnp.concatenate([c, c_ctx[None, :], jnp.zeros((mod_rows - b - 1, d), F32)], axis=0)
    mods = _modulation(cc, w_ada, b_ada)
    w_in_b = w_in.astype(BF16)
    w_out_b = w_out.astype(BF16)
    w_out_c = jnp.concatenate([w_out_b[:, 0:w_sc], w_out_b[:, 3 * w_sc:4 * w_sc]], axis=1)
    w_out_a = w_out_b[:, w_sc:3 * w_sc]
    bias = _bias_tables(rpb)
    fg = final_g.reshape(1, d)

    for l in range(depth):
        final = l == depth - 1
        shift = mods[l, :b, 0:d].reshape(b, 1, d)
        scale = mods[l, :b, d:2 * d].reshape(b, 1, d)
        gate = mods[l, :b, 2 * d:3 * d].reshape(b, 1, d)
        shift_c = mods[l, b:b + 1, 0:d].reshape(1, 1, d)
        scale_c = mods[l, b:b + 1, d:2 * d].reshape(1, 1, d)
        gate_c = mods[l, b:b + 1, 2 * d:3 * d].reshape(1, 1, d)
        g = norm_g[l].reshape(1, d)
        conv = (conv_sc[l], conv_cf[l]) + tuple(v[l].reshape(1, -1) for v in (conv_cf_b, ln_cf_g, ln_cf_b))

        zq_c, yc_c, zkv_c = _proj(ctx, g, scale_c, shift_c, w_in_b[l], *conv, w_out_c[l], tm=n_ctx)
        zq, yc, zkv = _proj(x, g, scale, shift, w_in_b[l], *conv, w_out_c[l], tm=TM_LATENT)
        x = _latent_mixer(x, gate, zq, yc, zkv, zkv_c, bias[l], w_out_a[l], fg, tm=TM_LATENT, final=final)
        if not final:
            ctx = _ctx_mixer(ctx, gate_c, zq_c, yc_c, zkv_c, w_out_a[l])
    return x
```

```python
import functools

import numpy as np
import jax
import jax.numpy as jnp
from jax import lax
from jax.experimental import pallas as pl
from jax.experimental.pallas import tpu as pltpu

F32 = jnp.float32
BF16 = jnp.bfloat16

GRID_W = 64
HEAD_DIM = 64
NA_ROWS = 8
NA_COLS = 16
SC_K = 3
CF_K = 31
NORM_EPS = 1e-6
LN_EPS = 1e-5
MASK_BIAS = -1e30

LANES = 128
SUBLANES = 8
MXU_N = 256
HALO_U = 16
LOG2_E = 1.4426950408889634
VMEM_LIMIT = 56 * 1024 * 1024
TM_LATENT = 512
PROBS_LAG = 2
OUTPUT_LAG = 5
OUT_GROUP = 256


def _silu(v):
    return v * (1.0 / (1.0 + jnp.exp(-v)))


def _sigmoid(v):
    return 1.0 / (1.0 + jnp.exp(-v))


def _mod_kernel(c_ref, w_ref, b_ref, o_ref):
    a = _silu(c_ref[...]).astype(BF16)
    o_ref[0] = jnp.dot(a, w_ref[0].astype(BF16), preferred_element_type=F32) + b_ref[0]


def _modulation(cc, w_ada, b_ada):
    depth, d, d3 = w_ada.shape
    rows = cc.shape[0]
    tn = 1024
    return pl.pallas_call(
        _mod_kernel,
        grid=(depth, d3 // tn),
        in_specs=[
            pl.BlockSpec((rows, d), lambda l, n: (0, 0)),
            pl.BlockSpec((1, d, tn), lambda l, n: (l, 0, n)),
            pl.BlockSpec((1, 1, tn), lambda l, n: (l, 0, n)),
        ],
        out_specs=pl.BlockSpec((1, rows, tn), lambda l, n: (l, 0, n)),
        out_shape=jax.ShapeDtypeStruct((depth, rows, d3), F32),
        compiler_params=pltpu.CompilerParams(vmem_limit_bytes=VMEM_LIMIT),
        name="adaln_mod",
    )(cc, w_ada, b_ada.reshape(depth, 1, d3))


class _DepthwiseConv:
    def __init__(self, ubuf, w_ref, lane0, w_lane0, width, ksize):
        self.ubuf, self.w_ref, self.lanes, self.ksize = ubuf, w_ref, slice(lane0, lane0 + width), ksize
        self.w_lanes = slice(w_lane0, w_lane0 + width)
        self.shifts = sorted({(j - ksize // 2) % SUBLANES for j in range(ksize)})
        self.rotated = {}
        self.row = lax.broadcasted_iota(jnp.int32, (SUBLANES, width), 0)

    def _rotated_partial(self, shift, row_tile):
        key = (shift, row_tile)
        if key not in self.rotated:
            acc = None
            for j in range(self.ksize):
                offset = j - self.ksize // 2
                if offset % SUBLANES != shift:
                    continue
                start = HALO_U + row_tile * SUBLANES + offset - shift
                term = self.ubuf[start:start + SUBLANES, self.lanes] * self.w_ref[j:j + 1, self.w_lanes]
                acc = term if acc is None else acc + term
            self.rotated[key] = acc if shift == 0 else pltpu.roll(acc, SUBLANES - shift, 0)
        return self.rotated[key]

    def tile(self, row_tile):
        out = None
        for shift in self.shifts:
            part = self._rotated_partial(shift, row_tile)
            if shift:
                part = jnp.where(self.row < SUBLANES - shift, part, self._rotated_partial(shift, row_tile + 1))
            out = part if out is None else out + part
        self.rotated = {k: v for k, v in self.rotated.items() if k[1] > row_tile}
        return out

    def rows(self, base, rows):
        return jnp.concatenate([self.tile(base // SUBLANES + k) for k in range(rows // SUBLANES)], axis=0)


def _conv_units(ubuf, gbuf, zcbuf, csc_ref, ccf_ref, cfb_ref, lng_ref, lnb_ref, tm, w_sc):
    chunk = 2 * SUBLANES
    conv_sc = [_DepthwiseConv(ubuf, csc_ref, l, l, LANES, SC_K) for l in range(0, w_sc, LANES)]
    conv_cf = [_DepthwiseConv(ubuf, ccf_ref, w_sc + l, l, LANES, CF_K) for l in range(0, w_sc, LANES)]

    def unit(c0):
        rows = slice(c0, c0 + chunk)
        ya = jnp.concatenate([cv.rows(c0, chunk) for cv in conv_sc], axis=-1) * gbuf[rows, 0:w_sc]
        zcbuf[rows, 0:w_sc] = ya.astype(BF16)
        u = jnp.concatenate([cv.rows(c0, chunk) for cv in conv_cf], axis=-1) + cfb_ref[...]
        mu = jnp.mean(u, axis=-1, keepdims=True)
        uc = u - mu
        var = jnp.mean(uc * uc, axis=-1, keepdims=True)
        un = uc * lax.rsqrt(var + LN_EPS) * lng_ref[...] + lnb_ref[...]
        yc = _silu(un) * gbuf[rows, w_sc:2 * w_sc]
        zcbuf[rows, w_sc:2 * w_sc] = yc.astype(BF16)

    return [functools.partial(unit, c0) for c0 in range(0, tm, chunk)]


def _proj_kernel(*refs, d, tm, has_halo):
    if has_halo:
        x_ref, xp_ref, xn_ref = refs[:3]
        refs = refs[3:]
    else:
        x_ref, refs = refs[0], refs[1:]
    (g_ref, scale_ref, shift_ref, w_ref, csc_ref, ccf_ref, cfb_ref, lng_ref, lnb_ref, woutc_ref,
     zq_ref, yc_ref, zkv_ref, hbuf, ubuf, gbuf, zcbuf) = refs
    w_sc = d // 4
    w_na = d // 2
    o_na = 4 * w_sc
    o_cf = o_na + 4 * w_na
    mul = g_ref[...] * (1.0 + scale_ref[0])

    def norm(x):
        ms = jnp.mean(x * x, axis=-1, keepdims=True)
        return (x * lax.rsqrt(ms + NORM_EPS) * mul + shift_ref[0]).astype(BF16)

    half = tm // 2
    for r in (0, half):
        hbuf[HALO_U + r:HALO_U + r + half, :] = norm(x_ref[0, r:r + half, :])
    tile = slice(HALO_U, HALO_U + tm)
    if has_halo:
        hbuf[0:HALO_U, :] = norm(xp_ref[0])
        hbuf[HALO_U + tm:2 * HALO_U + tm, :] = norm(xn_ref[0])
        wide = slice(0, tm + 2 * HALO_U)
        first = pl.program_id(1) == 0
        last = pl.program_id(1) == pl.num_programs(1) - 1
    else:
        wide = tile

    def proj(start, rows=tile):
        return jnp.dot(hbuf[rows, :], w_ref[:, start:start + MXU_N], preferred_element_type=F32)

    u = jnp.concatenate([proj(2 * w_sc, wide) * proj(0, wide),
                         proj(o_cf, wide) * _sigmoid(proj(o_cf + w_sc, wide))], axis=-1)
    zeros = jnp.zeros((HALO_U, 2 * w_sc), F32)
    if has_halo:
        ubuf[0:HALO_U, :] = jnp.where(first, zeros, u[0:HALO_U])
        ubuf[tile, :] = u[HALO_U:HALO_U + tm]
        ubuf[HALO_U + tm:2 * HALO_U + tm, :] = jnp.where(last, zeros, u[HALO_U + tm:2 * HALO_U + tm])
    else:
        ubuf[0:HALO_U, :] = zeros
        ubuf[tile, :] = u
        ubuf[HALO_U + tm:2 * HALO_U + tm, :] = zeros
    gbuf[:, 0:w_sc] = proj(w_sc) * _silu(proj(3 * w_sc))
    gbuf[:, w_sc:2 * w_sc] = _silu(proj(o_cf + 2 * w_sc))

    def store(ref, col, value):
        ref[0, :, col:col + MXU_N] = value.astype(BF16)

    def store_value_slabs(col, value):
        ones = jnp.ones((tm, LANES), BF16)
        for k in range(MXU_N // LANES):
            slab = w_na + (col // LANES + k) * MXU_N
            zkv_ref[0, :, slab:slab + LANES] = value[:, k * LANES:(k + 1) * LANES].astype(BF16)
            zkv_ref[0, :, slab + LANES:slab + MXU_N] = ones

    q_scale = HEAD_DIM ** -0.5 * LOG2_E
    attn_units = []
    for c in range(0, w_na, MXU_N):
        attn_units += [
            lambda c=c: store(zq_ref, c, proj(o_na + c) * q_scale),
            lambda c=c: store(zkv_ref, c, proj(o_na + w_na + c)),
            lambda c=c: store_value_slabs(c, proj(o_na + 2 * w_na + c)),
            lambda c=c: store(zq_ref, w_na + c, _silu(proj(o_na + 3 * w_na + c))),
        ]
    conv_units = _conv_units(ubuf, gbuf, zcbuf, csc_ref, ccf_ref, cfb_ref, lng_ref, lnb_ref, tm, w_sc)
    per = max(len(conv_units) // len(attn_units), 1)
    chunk = tm // len(conv_units)
    group = OUT_GROUP // chunk
    for k, conv_unit in enumerate(conv_units):
        if k % per == 0 and attn_units:
            attn_units.pop(0)()
        conv_unit()
        if (k + 1) % group == 0:
            rows = slice((k + 1 - group) * chunk, (k + 1) * chunk)
            yc_ref[0, rows, :] = jnp.dot(zcbuf[rows, :], woutc_ref[...], preferred_element_type=F32)
    for attn_unit in attn_units:
        attn_unit()


def _proj(x, g, scale, shift, w, csc, ccf, cfb, lng, lnb, woutc, *, tm):
    b, n, d = x.shape
    d_proj = w.shape[1]
    w_sc = d // 4
    nt = n // tm
    has_halo = nt > 1
    kv_width = d // 2 + (d // 2 // LANES) * MXU_N
    sb = 0 if scale.shape[0] == 1 else 1
    per = tm // HALO_U

    def const(shape):
        return pl.BlockSpec(shape, lambda bi, i: (0,) * len(shape), pipeline_mode=pl.Buffered(1))

    vec = pl.BlockSpec((1, 1, d), lambda bi, i: (bi * sb, 0, 0))
    out = lambda width: pl.BlockSpec((1, tm, width), lambda bi, i: (bi, i, 0))
    x_specs = [pl.BlockSpec((1, tm, d), lambda bi, i: (bi, i, 0))]
    x_args = [x]
    if has_halo:
        x_specs += [
            pl.BlockSpec((1, HALO_U, d), lambda bi, i: (bi, jnp.maximum(i * per - 1, 0), 0)),
            pl.BlockSpec((1, HALO_U, d), lambda bi, i: (bi, jnp.minimum((i + 1) * per, n // HALO_U - 1), 0)),
        ]
        x_args += [x, x]
    return pl.pallas_call(
        functools.partial(_proj_kernel, d=d, tm=tm, has_halo=has_halo),
        grid=(b, nt),
        in_specs=x_specs + [
            const((1, d)), vec, vec, const((d, d_proj)),
            const(csc.shape), const(ccf.shape), const(cfb.shape), const(lng.shape), const(lnb.shape),
            const(woutc.shape),
        ],
        out_specs=[out(d), out(d), out(kv_width)],
        out_shape=[
            jax.ShapeDtypeStruct((b, n, d), BF16),
            jax.ShapeDtypeStruct((b, n, d), F32),
            jax.ShapeDtypeStruct((b, n, kv_width), BF16),
        ],
        scratch_shapes=[
            pltpu.VMEM((tm + 2 * HALO_U, d), BF16),
            pltpu.VMEM((tm + 2 * HALO_U, 2 * w_sc), F32),
            pltpu.VMEM((tm, 2 * w_sc), F32),
            pltpu.VMEM((tm, 2 * w_sc), BF16),
        ],
        compiler_params=pltpu.CompilerParams(
            dimension_semantics=("parallel", "parallel"), vmem_limit_bytes=VMEM_LIMIT),
        name="proj_conv",
    )(*x_args, g, scale, shift, w, csc, ccf, cfb, lng, lnb, woutc)


def _split_heads(q2):
    lane = lax.broadcasted_iota(jnp.int32, q2.shape, 1)
    zero = jnp.zeros_like(q2)
    return jnp.concatenate([jnp.where(lane < HEAD_DIM, q2, zero),
                            jnp.where(lane >= HEAD_DIM, q2, zero)], axis=0)


def _merge_heads(o2, rows):
    lane = lax.broadcasted_iota(jnp.int32, (rows, LANES), 1)
    return jnp.where(lane < HEAD_DIM, o2[0:rows], o2[rows:2 * rows])


def _lane_tiles(a):
    return [a[:, i:i + LANES] for i in range(0, a.shape[1], LANES)]


_NT = (((1,), (1,)), ((), ()))


def _residual_out(x_ref, gate_ref, yc_ref, ybuf, wouta_ref):
    y = yc_ref[0] + jnp.dot(ybuf[...], wouta_ref[...], preferred_element_type=F32)
    return x_ref[0] + gate_ref[0] * y


def _latent_mixer_kernel(x_ref, gate_ref, zq_ref, yc_ref, kv_ref, kvc_ref, bias_ref, wouta_ref, fg_ref,
                         o_ref, ybuf, *, tm, d, rows_total, final):
    w_na = d // 2
    n_pairs = w_na // LANES
    tile_rows = tm // GRID_W
    kv_rows = kv_ref.shape[1] // GRID_W
    r0 = pl.program_id(1) * tile_rows
    kv_row0 = jnp.clip(r0 - NA_ROWS // 2, 0, rows_total - kv_rows)
    n_keys = NA_ROWS * GRID_W

    row_params = []
    for j in range(tile_rows):
        r = r0 + j
        rs = jnp.clip(r - NA_ROWS // 2, 0, rows_total - NA_ROWS)
        kstart = pl.multiple_of((rs - kv_row0) * GRID_W, GRID_W)
        row_params.append((kstart, rs - r + (NA_ROWS - 1)))

    def scores(it):
        j, p = divmod(it, n_pairs)
        kstart, bstart = row_params[j]
        lanes = slice(p * LANES, (p + 1) * LANES)
        qs = _split_heads(zq_ref[0, j * GRID_W:(j + 1) * GRID_W, lanes])
        kw = kv_ref[0, pl.ds(kstart, n_keys), lanes]
        kc = kvc_ref[0, :, lanes]
        s_w = lax.dot_general(qs, kw, _NT, preferred_element_type=F32)
        s_c = lax.dot_general(qs, kc, _NT, preferred_element_type=F32)
        bias = jnp.concatenate(
            [jnp.concatenate([bias_ref[2 * p, bstart + 2 * u], bias_ref[2 * p + 1, bstart + 2 * u]], axis=0)
             for u in range(NA_ROWS // 2)], axis=-1)
        s_w = s_w + bias
        m = jnp.max(functools.reduce(jnp.maximum, _lane_tiles(s_w) + _lane_tiles(s_c)), axis=-1, keepdims=True)
        return s_w, s_c, m

    def probs(s_w, s_c, m):
        return jnp.exp2(s_w - m).astype(BF16), jnp.exp2(s_c - m).astype(BF16)

    def outputs(it, p_w, p_c):
        j, p = divmod(it, n_pairs)
        kstart = row_params[j][0]
        slab = slice(w_na + p * MXU_N, w_na + (p + 1) * MXU_N)
        vw = kv_ref[0, pl.ds(kstart, n_keys), slab]
        vc = kvc_ref[0, :, slab]
        oa = jnp.dot(p_w, vw, preferred_element_type=F32) + jnp.dot(p_c, vc, preferred_element_type=F32)
        o = _merge_heads(oa[:, 0:LANES] / oa[:, LANES:MXU_N], GRID_W)
        rows = slice(j * GRID_W, (j + 1) * GRID_W)
        gate = zq_ref[0, rows, w_na + p * LANES:w_na + (p + 1) * LANES].astype(F32)
        ybuf[rows, p * LANES:(p + 1) * LANES] = (o * gate).astype(BF16)

    n_items = tile_rows * n_pairs
    st_scores, st_probs = {}, {}
    for k in range(n_items + OUTPUT_LAG):
        if k < n_items:
            st_scores[k] = scores(k)
        if 0 <= k - PROBS_LAG < n_items:
            st_probs[k - PROBS_LAG] = probs(*st_scores.pop(k - PROBS_LAG))
        if 0 <= k - OUTPUT_LAG < n_items:
            outputs(k - OUTPUT_LAG, *st_probs.pop(k - OUTPUT_LAG))

    xn = _residual_out(x_ref, gate_ref, yc_ref, ybuf, wouta_ref)
    if final:
        ms = jnp.mean(xn * xn, axis=-1, keepdims=True)
        xn = xn * lax.rsqrt(ms + NORM_EPS) * fg_ref[...]
    o_ref[0] = xn


def _ctx_mixer_kernel(x_ref, gate_ref, zq_ref, yc_ref, zkv_ref, wouta_ref,
                      o_ref, ybuf, *, tm, d):
    w_na = d // 2
    for p in range(w_na // LANES):
        lanes = slice(p * LANES, (p + 1) * LANES)
        qs = _split_heads(zq_ref[0, :, lanes])
        kc = zkv_ref[0, :, lanes]
        vc = zkv_ref[0, :, w_na + p * MXU_N:w_na + p * MXU_N + LANES]
        s = lax.dot_general(qs, kc, _NT, preferred_element_type=F32)
        m = jnp.max(s, axis=-1, keepdims=True)
        pr = jnp.exp2(s - m)
        den = jnp.sum(pr, axis=-1, keepdims=True)
        o2 = jnp.dot(pr.astype(BF16), vc, preferred_element_type=F32) / den
        o = _merge_heads(o2, tm)
        gate = zq_ref[0, :, w_na + p * LANES:w_na + (p + 1) * LANES].astype(F32)
        ybuf[:, lanes] = (o * gate).astype(BF16)
    o_ref[0] = _residual_out(x_ref, gate_ref, yc_ref, ybuf, wouta_ref)


def _latent_mixer(x, gate, zq, yc, zkv, zkv_c, bias, wouta, fg, *, tm, final):
    b, n, d = x.shape
    n_ctx, kv_width = zkv_c.shape[1], zkv.shape[2]
    nt = n // tm
    kv_window = tm + NA_ROWS * GRID_W
    halo = (NA_ROWS // 2) * GRID_W

    def const(shape):
        return pl.BlockSpec(shape, lambda bi, i: (0,) * len(shape), pipeline_mode=pl.Buffered(1))

    tile = lambda width: pl.BlockSpec((1, tm, width), lambda bi, i: (bi, i, 0))
    in_specs = [
        tile(d),
        pl.BlockSpec((1, 1, d), lambda bi, i: (bi, 0, 0)),
        tile(d), tile(d),
        pl.BlockSpec((pl.Element(1), pl.Element(kv_window), pl.Element(kv_width)),
                     lambda bi, i: (bi, halo * jnp.clip(i * (tm // halo) - 1, 0, (n - kv_window) // halo), 0)),
        pl.BlockSpec((1, n_ctx, kv_width), lambda bi, i: (bi, 0, 0)),
        const(bias.shape), const(wouta.shape), const(fg.shape),
    ]
    return pl.pallas_call(
        functools.partial(_latent_mixer_kernel, tm=tm, d=d, rows_total=n // GRID_W, final=final),
        grid=(b, nt),
        in_specs=in_specs,
        out_specs=tile(d),
        out_shape=jax.ShapeDtypeStruct((b, n, d), F32),
        scratch_shapes=[pltpu.VMEM((tm, d // 2), BF16)],
        compiler_params=pltpu.CompilerParams(
            dimension_semantics=("parallel", "parallel"), vmem_limit_bytes=VMEM_LIMIT),
        name="latent_mixer",
    )(x, gate, zq, yc, zkv, zkv_c, bias, wouta, fg)


def _ctx_mixer(x, gate, zq, yc, zkv, wouta):
    b, n, d = x.shape
    const2 = lambda shape: pl.BlockSpec(shape, lambda bi: (0, 0))
    tile = lambda width: pl.BlockSpec((1, n, width), lambda bi: (bi, 0, 0))
    return pl.pallas_call(
        functools.partial(_ctx_mixer_kernel, tm=n, d=d),
        grid=(b,),
        in_specs=[
            tile(d),
            pl.BlockSpec((1, 1, d), lambda bi: (0, 0, 0)),
            tile(d), tile(d), tile(zkv.shape[2]),
            const2(wouta.shape),
        ],
        out_specs=tile(d),
        out_shape=jax.ShapeDtypeStruct((b, n, d), F32),
        scratch_shapes=[pltpu.VMEM((n, d // 2), BF16)],
        compiler_params=pltpu.CompilerParams(
            dimension_semantics=("parallel",), vmem_limit_bytes=VMEM_LIMIT),
        name="ctx_mixer",
    )(x, gate, zq, yc, zkv, wouta)


def _bias_tables(rpb):
    cols = np.arange(GRID_W)
    col_start = np.clip(cols - NA_COLS // 2, 0, GRID_W - NA_COLS)
    inside = (cols[None, :] >= col_start[:, None]) & (cols[None, :] < col_start[:, None] + NA_COLS)
    rel = cols[None, :] - cols[:, None] + (NA_COLS - 1)
    select = (np.arange(2 * NA_COLS - 1)[:, None, None] == rel[None]) & inside[None]
    tab = jnp.einsum('lhdj,jqk->lhdqk', rpb * LOG2_E, jnp.asarray(select, F32),
                     precision=lax.Precision.HIGHEST)
    tab = tab + jnp.asarray(np.where(inside, 0.0, MASK_BIAS), F32)
    nxt = jnp.concatenate([tab[:, :, 1:], jnp.full_like(tab[:, :, :1], MASK_BIAS)], axis=2)
    return jnp.concatenate([tab, nxt], axis=-1).astype(F32)


def kernel(x, c, ctx, c_ctx, norm_g, w_ada, b_ada, w_in, conv_sc, rpb, conv_cf, conv_cf_b,
           ln_cf_g, ln_cf_b, w_out, final_g):
    depth = w_in.shape[0]
    b, n, d = x.shape
    n_ctx = ctx.shape[1]
    w_sc = d // 4
    mod_rows = 8 * ((b + 1 + 7) // 8)
    cc = jnp.concatenate([c, c_ctx[None, :], jnp.zeros((mod_rows - b - 1, d), F32)], axis=0)
    mods = _modulation(cc, w_ada, b_ada)
    w_in_b = w_in.astype(BF16)
    w_out_b = w_out.astype(BF16)
    w_out_c = jnp.concatenate([w_out_b[:, 0:w_sc], w_out_b[:, 3 * w_sc:4 * w_sc]], axis=1)
    w_out_a = w_out_b[:, w_sc:3 * w_sc]
    bias = _bias_tables(rpb)
    fg = final_g.reshape(1, d)

    for l in range(depth):
        final = l == depth - 1
        shift = mods[l, :b, 0:d].reshape(b, 1, d)
        scale = mods[l, :b, d:2 * d].reshape(b, 1, d)
        gate = mods[l, :b, 2 * d:3 * d].reshape(b, 1, d)
        shift_c = mods[l, b:b + 1, 0:d].reshape(1, 1, d)
        scale_c = mods[l, b:b + 1, d:2 * d].reshape(1, 1, d)
        gate_c = mods[l, b:b + 1, 2 * d:3 * d].reshape(1, 1, d)
        g = norm_g[l].reshape(1, d)
        conv = (conv_sc[l], conv_cf[l]) + tuple(v[l].reshape(1, -1) for v in (conv_cf_b, ln_cf_g, ln_cf_b))

        zq_c, yc_c, zkv_c = _proj(ctx, g, scale_c, shift_c, w_in_b[l], *conv, w_out_c[l], tm=n_ctx)
        zq, yc, zkv = _proj(x, g, scale, shift, w_in_b[l], *conv, w_out_c[l], tm=TM_LATENT)
        x = _latent_mixer(x, gate, zq, yc, zkv, zkv_c, bias[l], w_out_a[l], fg, tm=2 * TM_LATENT, final=final)
        if not final:
            ctx = _ctx_mixer(ctx, gate_c, zq_c, yc_c, zkv_c, w_out_a[l])
    return x
```

```python
import functools

import numpy as np
import jax
import jax.numpy as jnp
from jax import lax
from jax.experimental import pallas as pl
from jax.experimental.pallas import tpu as pltpu

F32 = jnp.float32
BF16 = jnp.bfloat16

GRID_W = 64
HEAD_DIM = 64
NA_ROWS = 8
NA_COLS = 16
SC_K = 3
CF_K = 31
NORM_EPS = 1e-6
LN_EPS = 1e-5
MASK_BIAS = -1e30

LANES = 128
SUBLANES = 8
MXU_N = 256
HALO_U = 16
LOG2_E = 1.4426950408889634
VMEM_LIMIT = 56 * 1024 * 1024
TM_LATENT = 512
PROBS_LAG = 2
OUTPUT_LAG = 5
OUT_GROUP = 256


def _silu(v):
    return v * (1.0 / (1.0 + jnp.exp(-v)))


def _sigmoid(v):
    return 1.0 / (1.0 + jnp.exp(-v))


def _mod_kernel(c_ref, w_ref, b_ref, o_ref):
    a = _silu(c_ref[...]).astype(BF16)
    o_ref[0] = jnp.dot(a, w_ref[0].astype(BF16), preferred_element_type=F32) + b_ref[0]


def _modulation(cc, w_ada, b_ada):
    depth, d, d3 = w_ada.shape
    rows = cc.shape[0]
    tn = 1024
    return pl.pallas_call(
        _mod_kernel,
        grid=(depth, d3 // tn),
        in_specs=[
            pl.BlockSpec((rows, d), lambda l, n: (0, 0)),
            pl.BlockSpec((1, d, tn), lambda l, n: (l, 0, n)),
            pl.BlockSpec((1, 1, tn), lambda l, n: (l, 0, n)),
        ],
        out_specs=pl.BlockSpec((1, rows, tn), lambda l, n: (l, 0, n)),
        out_shape=jax.ShapeDtypeStruct((depth, rows, d3), F32),
        compiler_params=pltpu.CompilerParams(vmem_limit_bytes=VMEM_LIMIT),
        name="adaln_mod",
    )(cc, w_ada, b_ada.reshape(depth, 1, d3))


class _DepthwiseConv:
    def __init__(self, ubuf, w_ref, lane0, w_lane0, width, ksize):
        self.ubuf, self.w_ref, self.lanes, self.ksize = ubuf, w_ref, slice(lane0, lane0 + width), ksize
        self.w_lanes = slice(w_lane0, w_lane0 + width)
        self.shifts = sorted({(j - ksize // 2) % SUBLANES for j in range(ksize)})
        self.rotated = {}
        self.row = lax.broadcasted_iota(jnp.int32, (SUBLANES, width), 0)

    def _rotated_partial(self, shift, row_tile):
        key = (shift, row_tile)
        if key not in self.rotated:
            acc = None
            for j in range(self.ksize):
                offset = j - self.ksize // 2
                if offset % SUBLANES != shift:
                    continue
                start = HALO_U + row_tile * SUBLANES + offset - shift
                term = self.ubuf[start:start + SUBLANES, self.lanes] * self.w_ref[j:j + 1, self.w_lanes]
                acc = term if acc is None else acc + term
            self.rotated[key] = acc if shift == 0 else pltpu.roll(acc, SUBLANES - shift, 0)
        return self.rotated[key]

    def tile(self, row_tile):
        out = None
        for shift in self.shifts:
            part = self._rotated_partial(shift, row_tile)
            if shift:
                part = jnp.where(self.row < SUBLANES - shift, part, self._rotated_partial(shift, row_tile + 1))
            out = part if out is None else out + part
        self.rotated = {k: v for k, v in self.rotated.items() if k[1] > row_tile}
        return out

    def rows(self, base, rows):
        return jnp.concatenate([self.tile(base // SUBLANES + k) for k in range(rows // SUBLANES)], axis=0)


def _conv_units(ubuf, gbuf, zcbuf, csc_ref, ccf_ref, cfb_ref, lng_ref, lnb_ref, tm, w_sc):
    chunk = 2 * SUBLANES
    conv_sc = [_DepthwiseConv(ubuf, csc_ref, l, l, LANES, SC_K) for l in range(0, w_sc, LANES)]
    conv_cf = [_DepthwiseConv(ubuf, ccf_ref, w_sc + l, l, LANES, CF_K) for l in range(0, w_sc, LANES)]

    def unit(c0):
        rows = slice(c0, c0 + chunk)
        ya = jnp.concatenate([cv.rows(c0, chunk) for cv in conv_sc], axis=-1) * gbuf[rows, 0:w_sc]
        zcbuf[rows, 0:w_sc] = ya.astype(BF16)
        u = jnp.concatenate([cv.rows(c0, chunk) for cv in conv_cf], axis=-1) + cfb_ref[...]
        mu = jnp.mean(u, axis=-1, keepdims=True)
        uc = u - mu
        var = jnp.mean(uc * uc, axis=-1, keepdims=True)
        un = uc * lax.rsqrt(var + LN_EPS) * lng_ref[...] + lnb_ref[...]
        yc = _silu(un) * gbuf[rows, w_sc:2 * w_sc]
        zcbuf[rows, w_sc:2 * w_sc] = yc.astype(BF16)

    return [functools.partial(unit, c0) for c0 in range(0, tm, chunk)]


def _proj_kernel(*refs, d, tm, has_halo):
    if has_halo:
        x_ref, xp_ref, xn_ref = refs[:3]
        refs = refs[3:]
    else:
        x_ref, refs = refs[0], refs[1:]
    (g_ref, scale_ref, shift_ref, w_ref, csc_ref, ccf_ref, cfb_ref, lng_ref, lnb_ref, woutc_ref,
     zq_ref, yc_ref, zkv_ref, hbuf, ubuf, gbuf, zcbuf) = refs
    w_sc = d // 4
    w_na = d // 2
    o_na = 4 * w_sc
    o_cf = o_na + 4 * w_na
    mul = g_ref[...] * (1.0 + scale_ref[0])

    def norm(x):
        ms = jnp.mean(x * x, axis=-1, keepdims=True)
        return (x * lax.rsqrt(ms + NORM_EPS) * mul + shift_ref[0]).astype(BF16)

    half = tm // 2
    for r in (0, half):
        hbuf[HALO_U + r:HALO_U + r + half, :] = norm(x_ref[0, r:r + half, :])
    tile = slice(HALO_U, HALO_U + tm)
    if has_halo:
        hbuf[0:HALO_U, :] = norm(xp_ref[0])
        hbuf[HALO_U + tm:2 * HALO_U + tm, :] = norm(xn_ref[0])
        wide = slice(0, tm + 2 * HALO_U)
        first = pl.program_id(1) == 0
        last = pl.program_id(1) == pl.num_programs(1) - 1
    else:
        wide = tile

    def proj(start, rows=tile):
        return jnp.dot(hbuf[rows, :], w_ref[:, start:start + MXU_N], preferred_element_type=F32)

    u = jnp.concatenate([proj(2 * w_sc, wide) * proj(0, wide),
                         proj(o_cf, wide) * _sigmoid(proj(o_cf + w_sc, wide))], axis=-1)
    zeros = jnp.zeros((HALO_U, 2 * w_sc), F32)
    if has_halo:
        ubuf[0:HALO_U, :] = jnp.where(first, zeros, u[0:HALO_U])
        ubuf[tile, :] = u[HALO_U:HALO_U + tm]
        ubuf[HALO_U + tm:2 * HALO_U + tm, :] = jnp.where(last, zeros, u[HALO_U + tm:2 * HALO_U + tm])
    else:
        ubuf[0:HALO_U, :] = zeros
        ubuf[tile, :] = u
        ubuf[HALO_U + tm:2 * HALO_U + tm, :] = zeros
    gbuf[:, 0:w_sc] = proj(w_sc) * _silu(proj(3 * w_sc))
    gbuf[:, w_sc:2 * w_sc] = _silu(proj(o_cf + 2 * w_sc))

    def store(ref, col, value):
        ref[0, :, col:col + MXU_N] = value.astype(BF16)

    def store_value_slabs(col, value):
        ones = jnp.ones((tm, LANES), BF16)
        for k in range(MXU_N // LANES):
            slab = w_na + (col // LANES + k) * MXU_N
            zkv_ref[0, :, slab:slab + LANES] = value[:, k * LANES:(k + 1) * LANES].astype(BF16)
            zkv_ref[0, :, slab + LANES:slab + MXU_N] = ones

    q_scale = HEAD_DIM ** -0.5 * LOG2_E
    attn_units = []
    for c in range(0, w_na, MXU_N):
        attn_units += [
            lambda c=c: store(zq_ref, c, proj(o_na + c) * q_scale),
            lambda c=c: store(zkv_ref, c, proj(o_na + w_na + c)),
            lambda c=c: store_value_slabs(c, proj(o_na + 2 * w_na + c)),
            lambda c=c: store(zq_ref, w_na + c, _silu(proj(o_na + 3 * w_na + c))),
        ]
    conv_units = _conv_units(ubuf, gbuf, zcbuf, csc_ref, ccf_ref, cfb_ref, lng_ref, lnb_ref, tm, w_sc)
    per = max(len(conv_units) // len(attn_units), 1)
    chunk = tm // len(conv_units)
    group = OUT_GROUP // chunk
    for k, conv_unit in enumerate(conv_units):
        if k % per == 0 and attn_units:
            attn_units.pop(0)()
        conv_unit()
        if (k + 1) % group == 0:
            rows = slice((k + 1 - group) * chunk, (k + 1) * chunk)
            yc_ref[0, rows, :] = jnp.dot(zcbuf[rows, :], woutc_ref[...], preferred_element_type=F32)
    for attn_unit in attn_units:
        attn_unit()


def _proj(x, g, scale, shift, w, csc, ccf, cfb, lng, lnb, woutc, *, tm):
    b, n, d = x.shape
    d_proj = w.shape[1]
    w_sc = d // 4
    nt = n // tm
    has_halo = nt > 1
    kv_width = d // 2 + (d // 2 // LANES) * MXU_N
    sb = 0 if scale.shape[0] == 1 else 1
    per = tm // HALO_U

    def const(shape):
        return pl.BlockSpec(shape, lambda bi, i: (0,) * len(shape), pipeline_mode=pl.Buffered(1))

    vec = pl.BlockSpec((1, 1, d), lambda bi, i: (bi * sb, 0, 0))
    out = lambda width: pl.BlockSpec((1, tm, width), lambda bi, i: (bi, i, 0))
    x_specs = [pl.BlockSpec((1, tm, d), lambda bi, i: (bi, i, 0))]
    x_args = [x]
    if has_halo:
        x_specs += [
            pl.BlockSpec((1, HALO_U, d), lambda bi, i: (bi, jnp.maximum(i * per - 1, 0), 0)),
            pl.BlockSpec((1, HALO_U, d), lambda bi, i: (bi, jnp.minimum((i + 1) * per, n // HALO_U - 1), 0)),
        ]
        x_args += [x, x]
    return pl.pallas_call(
        functools.partial(_proj_kernel, d=d, tm=tm, has_halo=has_halo),
        grid=(b, nt),
        in_specs=x_specs + [
            const((1, d)), vec, vec, const((d, d_proj)),
            const(csc.shape), const(ccf.shape), const(cfb.shape), const(lng.shape), const(lnb.shape),
            const(woutc.shape),
        ],
        out_specs=[out(d), out(d), out(kv_width)],
        out_shape=[
            jax.ShapeDtypeStruct((b, n, d), BF16),
            jax.ShapeDtypeStruct((b, n, d), F32),
            jax.ShapeDtypeStruct((b, n, kv_width), BF16),
        ],
        scratch_shapes=[
            pltpu.VMEM((tm + 2 * HALO_U, d), BF16),
            pltpu.VMEM((tm + 2 * HALO_U, 2 * w_sc), F32),
            pltpu.VMEM((tm, 2 * w_sc), F32),
            pltpu.VMEM((tm, 2 * w_sc), BF16),
        ],
        compiler_params=pltpu.CompilerParams(
            dimension_semantics=("parallel", "parallel"), vmem_limit_bytes=VMEM_LIMIT),
        name="proj_conv",
    )(*x_args, g, scale, shift, w, csc, ccf, cfb, lng, lnb, woutc)


def _kv_proj_kernel(x_ref, g_ref, scale_ref, shift_ref, w_ref, zkv_ref, *, d):
    w_na = d // 2
    x = x_ref[0]
    ms = jnp.mean(x * x, axis=-1, keepdims=True)
    h = (x * lax.rsqrt(ms + NORM_EPS) * (g_ref[...] * (1.0 + scale_ref[0])) + shift_ref[0]).astype(BF16)
    ones = jnp.ones((x.shape[0], LANES), BF16)
    for c in range(0, w_na, MXU_N):
        k = jnp.dot(h, w_ref[:, d + w_na + c:d + w_na + c + MXU_N], preferred_element_type=F32)
        v = jnp.dot(h, w_ref[:, d + 2 * w_na + c:d + 2 * w_na + c + MXU_N], preferred_element_type=F32)
        zkv_ref[0, :, c:c + MXU_N] = k.astype(BF16)
        for t in range(MXU_N // LANES):
            slab = w_na + (c // LANES + t) * MXU_N
            zkv_ref[0, :, slab:slab + LANES] = v[:, t * LANES:(t + 1) * LANES].astype(BF16)
            zkv_ref[0, :, slab + LANES:slab + MXU_N] = ones


def _kv_proj(x, g, scale, shift, w):
    b, n, d = x.shape
    kv_width = d // 2 + (d // 2 // LANES) * MXU_N
    vec = pl.BlockSpec((1, 1, d), lambda bi: (0, 0, 0))
    return pl.pallas_call(
        functools.partial(_kv_proj_kernel, d=d),
        grid=(b,),
        in_specs=[
            pl.BlockSpec((1, n, d), lambda bi: (bi, 0, 0)),
            pl.BlockSpec((1, d), lambda bi: (0, 0)),
            vec, vec,
            pl.BlockSpec(w.shape, lambda bi: (0, 0)),
        ],
        out_specs=pl.BlockSpec((1, n, kv_width), lambda bi: (bi, 0, 0)),
        out_shape=jax.ShapeDtypeStruct((b, n, kv_width), BF16),
        compiler_params=pltpu.CompilerParams(
            dimension_semantics=("parallel",), vmem_limit_bytes=VMEM_LIMIT),
        name="ctx_kv_proj",
    )(x, g, scale, shift, w)


def _split_heads(q2):
    lane = lax.broadcasted_iota(jnp.int32, q2.shape, 1)
    zero = jnp.zeros_like(q2)
    return jnp.concatenate([jnp.where(lane < HEAD_DIM, q2, zero),
                            jnp.where(lane >= HEAD_DIM, q2, zero)], axis=0)


def _merge_heads(o2, rows):
    lane = lax.broadcasted_iota(jnp.int32, (rows, LANES), 1)
    return jnp.where(lane < HEAD_DIM, o2[0:rows], o2[rows:2 * rows])


def _lane_tiles(a):
    return [a[:, i:i + LANES] for i in range(0, a.shape[1], LANES)]


_NT = (((1,), (1,)), ((), ()))


def _residual_out(x_ref, gate_ref, yc_ref, ybuf, wouta_ref):
    y = yc_ref[0] + jnp.dot(ybuf[...], wouta_ref[...], preferred_element_type=F32)
    return x_ref[0] + gate_ref[0] * y


def _latent_mixer_kernel(x_ref, gate_ref, zq_ref, yc_ref, kv_ref, kvc_ref, bias_ref, wouta_ref, fg_ref,
                         o_ref, ybuf, *, tm, d, rows_total, final):
    w_na = d // 2
    n_pairs = w_na // LANES
    tile_rows = tm // GRID_W
    kv_rows = kv_ref.shape[1] // GRID_W
    r0 = pl.program_id(1) * tile_rows
    kv_row0 = jnp.clip(r0 - NA_ROWS // 2, 0, rows_total - kv_rows)
    n_keys = NA_ROWS * GRID_W

    row_params = []
    for j in range(tile_rows):
        r = r0 + j
        rs = jnp.clip(r - NA_ROWS // 2, 0, rows_total - NA_ROWS)
        kstart = pl.multiple_of((rs - kv_row0) * GRID_W, GRID_W)
        row_params.append((kstart, rs - r + (NA_ROWS - 1)))

    def scores(it):
        j, p = divmod(it, n_pairs)
        kstart, bstart = row_params[j]
        lanes = slice(p * LANES, (p + 1) * LANES)
        qs = _split_heads(zq_ref[0, j * GRID_W:(j + 1) * GRID_W, lanes])
        kw = kv_ref[0, pl.ds(kstart, n_keys), lanes]
        kc = kvc_ref[0, :, lanes]
        s_w = lax.dot_general(qs, kw, _NT, preferred_element_type=F32)
        s_c = lax.dot_general(qs, kc, _NT, preferred_element_type=F32)
        bias = jnp.concatenate(
            [jnp.concatenate([bias_ref[2 * p, bstart + 2 * u], bias_ref[2 * p + 1, bstart + 2 * u]], axis=0)
             for u in range(NA_ROWS // 2)], axis=-1)
        s_w = s_w + bias
        m = jnp.max(functools.reduce(jnp.maximum, _lane_tiles(s_w) + _lane_tiles(s_c)), axis=-1, keepdims=True)
        return s_w, s_c, m

    def probs(s_w, s_c, m):
        return jnp.exp2(s_w - m).astype(BF16), jnp.exp2(s_c - m).astype(BF16)

    def outputs(it, p_w, p_c):
        j, p = divmod(it, n_pairs)
        kstart = row_params[j][0]
        slab = slice(w_na + p * MXU_N, w_na + (p + 1) * MXU_N)
        vw = kv_ref[0, pl.ds(kstart, n_keys), slab]
        vc = kvc_ref[0, :, slab]
        oa = jnp.dot(p_w, vw, preferred_element_type=F32) + jnp.dot(p_c, vc, preferred_element_type=F32)
        o = _merge_heads(oa[:, 0:LANES] / oa[:, LANES:MXU_N], GRID_W)
        rows = slice(j * GRID_W, (j + 1) * GRID_W)
        gate = zq_ref[0, rows, w_na + p * LANES:w_na + (p + 1) * LANES].astype(F32)
        ybuf[rows, p * LANES:(p + 1) * LANES] = (o * gate).astype(BF16)

    n_items = tile_rows * n_pairs
    st_scores, st_probs = {}, {}
    for k in range(n_items + OUTPUT_LAG):
        if k < n_items:
            st_scores[k] = scores(k)
        if 0 <= k - PROBS_LAG < n_items:
            st_probs[k - PROBS_LAG] = probs(*st_scores.pop(k - PROBS_LAG))
        if 0 <= k - OUTPUT_LAG < n_items:
            outputs(k - OUTPUT_LAG, *st_probs.pop(k - OUTPUT_LAG))

    xn = _residual_out(x_ref, gate_ref, yc_ref, ybuf, wouta_ref)
    if final:
        ms = jnp.mean(xn * xn, axis=-1, keepdims=True)
        xn = xn * lax.rsqrt(ms + NORM_EPS) * fg_ref[...]
    o_ref[0] = xn


def _ctx_mixer_kernel(x_ref, gate_ref, zq_ref, yc_ref, zkv_ref, wouta_ref,
                      o_ref, ybuf, *, tm, d):
    w_na = d // 2
    for p in range(w_na // LANES):
        lanes = slice(p * LANES, (p + 1) * LANES)
        qs = _split_heads(zq_ref[0, :, lanes])
        kc = zkv_ref[0, :, lanes]
        vc = zkv_ref[0, :, w_na + p * MXU_N:w_na + p * MXU_N + LANES]
        s = lax.dot_general(qs, kc, _NT, preferred_element_type=F32)
        m = jnp.max(s, axis=-1, keepdims=True)
        pr = jnp.exp2(s - m)
        den = jnp.sum(pr, axis=-1, keepdims=True)
        o2 = jnp.dot(pr.astype(BF16), vc, preferred_element_type=F32) / den
        o = _merge_heads(o2, tm)
        gate = zq_ref[0, :, w_na + p * LANES:w_na + (p + 1) * LANES].astype(F32)
        ybuf[:, lanes] = (o * gate).astype(BF16)
    o_ref[0] = _residual_out(x_ref, gate_ref, yc_ref, ybuf, wouta_ref)


def _latent_mixer(x, gate, zq, yc, zkv, zkv_c, bias, wouta, fg, *, tm, final):
    b, n, d = x.shape
    n_ctx, kv_width = zkv_c.shape[1], zkv.shape[2]
    nt = n // tm
    kv_window = tm + NA_ROWS * GRID_W
    halo = (NA_ROWS // 2) * GRID_W

    def const(shape):
        return pl.BlockSpec(shape, lambda bi, i: (0,) * len(shape), pipeline_mode=pl.Buffered(1))

    tile = lambda width: pl.BlockSpec((1, tm, width), lambda bi, i: (bi, i, 0))
    in_specs = [
        tile(d),
        pl.BlockSpec((1, 1, d), lambda bi, i: (bi, 0, 0)),
        tile(d), tile(d),
        pl.BlockSpec((pl.Element(1), pl.Element(kv_window), pl.Element(kv_width)),
                     lambda bi, i: (bi, halo * jnp.clip(i * (tm // halo) - 1, 0, (n - kv_window) // halo), 0)),
        pl.BlockSpec((1, n_ctx, kv_width), lambda bi, i: (bi, 0, 0)),
        const(bias.shape), const(wouta.shape), const(fg.shape),
    ]
    return pl.pallas_call(
        functools.partial(_latent_mixer_kernel, tm=tm, d=d, rows_total=n // GRID_W, final=final),
        grid=(b, nt),
        in_specs=in_specs,
        out_specs=tile(d),
        out_shape=jax.ShapeDtypeStruct((b, n, d), F32),
        scratch_shapes=[pltpu.VMEM((tm, d // 2), BF16)],
        compiler_params=pltpu.CompilerParams(
            dimension_semantics=("parallel", "parallel"), vmem_limit_bytes=VMEM_LIMIT),
        name="latent_mixer",
    )(x, gate, zq, yc, zkv, zkv_c, bias, wouta, fg)


def _ctx_mixer(x, gate, zq, yc, zkv, wouta):
    b, n, d = x.shape
    const2 = lambda shape: pl.BlockSpec(shape, lambda bi: (0, 0))
    tile = lambda width: pl.BlockSpec((1, n, width), lambda bi: (bi, 0, 0))
    return pl.pallas_call(
        functools.partial(_ctx_mixer_kernel, tm=n, d=d),
        grid=(b,),
        in_specs=[
            tile(d),
            pl.BlockSpec((1, 1, d), lambda bi: (0, 0, 0)),
            tile(d), tile(d), tile(zkv.shape[2]),
            const2(wouta.shape),
        ],
        out_specs=tile(d),
        out_shape=jax.ShapeDtypeStruct((b, n, d), F32),
        scratch_shapes=[pltpu.VMEM((n, d // 2), BF16)],
        compiler_params=pltpu.CompilerParams(
            dimension_semantics=("parallel",), vmem_limit_bytes=VMEM_LIMIT),
        name="ctx_mixer",
    )(x, gate, zq, yc, zkv, wouta)


def _bias_tables(rpb):
    cols = np.arange(GRID_W)
    col_start = np.clip(cols - NA_COLS // 2, 0, GRID_W - NA_COLS)
    inside = (cols[None, :] >= col_start[:, None]) & (cols[None, :] < col_start[:, None] + NA_COLS)
    rel = cols[None, :] - cols[:, None] + (NA_COLS - 1)
    select = (np.arange(2 * NA_COLS - 1)[:, None, None] == rel[None]) & inside[None]
    tab = jnp.einsum('lhdj,jqk->lhdqk', rpb * LOG2_E, jnp.asarray(select, F32),
                     precision=lax.Precision.HIGHEST)
    tab = tab + jnp.asarray(np.where(inside, 0.0, MASK_BIAS), F32)
    nxt = jnp.concatenate([tab[:, :, 1:], jnp.full_like(tab[:, :, :1], MASK_BIAS)], axis=2)
    return jnp.concatenate([tab, nxt], axis=-1).astype(F32)


def kernel(x, c, ctx, c_ctx, norm_g, w_ada, b_ada, w_in, conv_sc, rpb, conv_cf, conv_cf_b,
           ln_cf_g, ln_cf_b, w_out, final_g):
    depth = w_in.shape[0]
    b, n, d = x.shape
    n_ctx = ctx.shape[1]
    w_sc = d // 4
    mod_rows = 8 * ((b + 1 + 7) // 8)
    cc = jnp.concatenate([c, c_ctx[None, :], jnp.zeros((mod_rows - b - 1, d), F32)], axis=0)
    mods = _modulation(cc, w_ada, b_ada)
    w_in_b = w_in.astype(BF16)
    w_out_b = w_out.astype(BF16)
    w_out_c = jnp.concatenate([w_out_b[:, 0:w_sc], w_out_b[:, 3 * w_sc:4 * w_sc]], axis=1)
    w_out_a = w_out_b[:, w_sc:3 * w_sc]
    bias = _bias_tables(rpb)
    fg = final_g.reshape(1, d)

    for l in range(depth):
        final = l == depth - 1
        shift = mods[l, :b, 0:d].reshape(b, 1, d)
        scale = mods[l, :b, d:2 * d].reshape(b, 1, d)
        gate = mods[l, :b, 2 * d:3 * d].reshape(b, 1, d)
        shift_c = mods[l, b:b + 1, 0:d].reshape(1, 1, d)
        scale_c = mods[l, b:b + 1, d:2 * d].reshape(1, 1, d)
        gate_c = mods[l, b:b + 1, 2 * d:3 * d].reshape(1, 1, d)
        g = norm_g[l].reshape(1, d)
        conv = (conv_sc[l], conv_cf[l]) + tuple(v[l].reshape(1, -1) for v in (conv_cf_b, ln_cf_g, ln_cf_b))

        if final:
            zkv_c = _kv_proj(ctx, g, scale_c, shift_c, w_in_b[l])
        else:
            zq_c, yc_c, zkv_c = _proj(ctx, g, scale_c, shift_c, w_in_b[l], *conv, w_out_c[l], tm=n_ctx)
        zq, yc, zkv = _proj(x, g, scale, shift, w_in_b[l], *conv, w_out_c[l], tm=2 * TM_LATENT)
        x = _latent_mixer(x, gate, zq, yc, zkv, zkv_c, bias[l], w_out_a[l], fg, tm=2 * TM_LATENT, final=final)
        if not final:
            ctx = _ctx_mixer(ctx, gate_c, zq_c, yc_c, zkv_c, w_out_a[l])
    return x
```

```python
import functools

import numpy as np
import jax
import jax.numpy as jnp
from jax import lax
from jax.experimental import pallas as pl
from jax.experimental.pallas import tpu as pltpu

F32 = jnp.float32
BF16 = jnp.bfloat16

GRID_W = 64
HEAD_DIM = 64
NA_ROWS = 8
NA_COLS = 16
SC_K = 3
CF_K = 31
NORM_EPS = 1e-6
LN_EPS = 1e-5
MASK_BIAS = -1e30

LANES = 128
SUBLANES = 8
MXU_N = 256
HALO_U = 16
LOG2_E = 1.4426950408889634
VMEM_LIMIT = 56 * 1024 * 1024
TM_LATENT = 512
PROBS_LAG = 2
OUTPUT_LAG = 5
OUT_GROUP = 256


def _silu(v):
    return v * (1.0 / (1.0 + jnp.exp(-v)))


def _layer_block(stacked, layer):
    zeros = (0,) * (stacked.ndim - 1)
    return pl.BlockSpec((None,) + stacked.shape[1:], lambda *_: (layer,) + zeros, pipeline_mode=pl.Buffered(1))


def _sigmoid(v):
    return 1.0 / (1.0 + jnp.exp(-v))


def _mod_kernel(c_ref, w_ref, b_ref, o_ref):
    a = _silu(c_ref[...]).astype(BF16)
    o_ref[0] = jnp.dot(a, w_ref[0].astype(BF16), preferred_element_type=F32) + b_ref[0]


def _modulation(cc, w_ada, b_ada):
    depth, d, d3 = w_ada.shape
    rows = cc.shape[0]
    tn = 1024
    return pl.pallas_call(
        _mod_kernel,
        grid=(depth, d3 // tn),
        in_specs=[
            pl.BlockSpec((rows, d), lambda l, n: (0, 0)),
            pl.BlockSpec((1, d, tn), lambda l, n: (l, 0, n)),
            pl.BlockSpec((1, 1, tn), lambda l, n: (l, 0, n)),
        ],
        out_specs=pl.BlockSpec((1, rows, tn), lambda l, n: (l, 0, n)),
        out_shape=jax.ShapeDtypeStruct((depth, rows, d3), F32),
        compiler_params=pltpu.CompilerParams(vmem_limit_bytes=VMEM_LIMIT),
        name="adaln_mod",
    )(cc, w_ada, b_ada.reshape(depth, 1, d3))


class _DepthwiseConv:
    def __init__(self, ubuf, w_ref, lane0, w_lane0, width, ksize):
        self.ubuf, self.w_ref, self.lanes, self.ksize = ubuf, w_ref, slice(lane0, lane0 + width), ksize
        self.w_lanes = slice(w_lane0, w_lane0 + width)
        self.shifts = sorted({(j - ksize // 2) % SUBLANES for j in range(ksize)})
        self.rotated = {}
        self.row = lax.broadcasted_iota(jnp.int32, (SUBLANES, width), 0)

    def _rotated_partial(self, shift, row_tile):
        key = (shift, row_tile)
        if key not in self.rotated:
            acc = None
            for j in range(self.ksize):
                offset = j - self.ksize // 2
                if offset % SUBLANES != shift:
                    continue
                start = HALO_U + row_tile * SUBLANES + offset - shift
                term = self.ubuf[start:start + SUBLANES, self.lanes] * self.w_ref[j:j + 1, self.w_lanes]
                acc = term if acc is None else acc + term
            self.rotated[key] = acc if shift == 0 else pltpu.roll(acc, SUBLANES - shift, 0)
        return self.rotated[key]

    def tile(self, row_tile):
        out = None
        for shift in self.shifts:
            part = self._rotated_partial(shift, row_tile)
            if shift:
                part = jnp.where(self.row < SUBLANES - shift, part, self._rotated_partial(shift, row_tile + 1))
            out = part if out is None else out + part
        self.rotated = {k: v for k, v in self.rotated.items() if k[1] > row_tile}
        return out

    def rows(self, base, rows):
        return jnp.concatenate([self.tile(base // SUBLANES + k) for k in range(rows // SUBLANES)], axis=0)


def _conv_units(ubuf, gbuf, zcbuf, csc_ref, ccf_ref, cfb_ref, lng_ref, lnb_ref, tm, w_sc):
    chunk = 2 * SUBLANES
    conv_sc = [_DepthwiseConv(ubuf, csc_ref, l, l, LANES, SC_K) for l in range(0, w_sc, LANES)]
    conv_cf = [_DepthwiseConv(ubuf, ccf_ref, w_sc + l, l, LANES, CF_K) for l in range(0, w_sc, LANES)]

    def unit(c0):
        rows = slice(c0, c0 + chunk)
        ya = jnp.concatenate([cv.rows(c0, chunk) for cv in conv_sc], axis=-1) * gbuf[rows, 0:w_sc]
        zcbuf[rows, 0:w_sc] = ya.astype(BF16)
        u = jnp.concatenate([cv.rows(c0, chunk) for cv in conv_cf], axis=-1) + cfb_ref[...]
        mu = jnp.mean(u, axis=-1, keepdims=True)
        uc = u - mu
        var = jnp.mean(uc * uc, axis=-1, keepdims=True)
        un = uc * lax.rsqrt(var + LN_EPS) * lng_ref[...] + lnb_ref[...]
        yc = _silu(un) * gbuf[rows, w_sc:2 * w_sc]
        zcbuf[rows, w_sc:2 * w_sc] = yc.astype(BF16)

    return [functools.partial(unit, c0) for c0 in range(0, tm, chunk)]


def _proj_kernel(*refs, d, tm, has_halo):
    if has_halo:
        x_ref, xp_ref, xn_ref = refs[:3]
        refs = refs[3:]
    else:
        x_ref, refs = refs[0], refs[1:]
    (g_ref, scale_ref, shift_ref, w_ref, csc_ref, ccf_ref, cfb_ref, lng_ref, lnb_ref, woutc_ref,
     zq_ref, yc_ref, zkv_ref, hbuf, ubuf, gbuf, zcbuf) = refs
    w_sc = d // 4
    w_na = d // 2
    o_na = 4 * w_sc
    o_cf = o_na + 4 * w_na
    mul = g_ref[...] * (1.0 + scale_ref[0])

    def norm(x):
        ms = jnp.mean(x * x, axis=-1, keepdims=True)
        return (x * lax.rsqrt(ms + NORM_EPS) * mul + shift_ref[0]).astype(BF16)

    half = tm // 2
    for r in (0, half):
        hbuf[HALO_U + r:HALO_U + r + half, :] = norm(x_ref[0, r:r + half, :])
    tile = slice(HALO_U, HALO_U + tm)
    if has_halo:
        hbuf[0:HALO_U, :] = norm(xp_ref[0])
        hbuf[HALO_U + tm:2 * HALO_U + tm, :] = norm(xn_ref[0])
        wide = slice(0, tm + 2 * HALO_U)
        first = pl.program_id(1) == 0
        last = pl.program_id(1) == pl.num_programs(1) - 1
    else:
        wide = tile

    def proj(start, rows=tile):
        return jnp.dot(hbuf[rows, :], w_ref[:, start:start + MXU_N], preferred_element_type=F32)

    u = jnp.concatenate([proj(2 * w_sc, wide) * proj(0, wide),
                         proj(o_cf, wide) * _sigmoid(proj(o_cf + w_sc, wide))], axis=-1)
    zeros = jnp.zeros((HALO_U, 2 * w_sc), F32)
    if has_halo:
        ubuf[0:HALO_U, :] = jnp.where(first, zeros, u[0:HALO_U])
        ubuf[tile, :] = u[HALO_U:HALO_U + tm]
        ubuf[HALO_U + tm:2 * HALO_U + tm, :] = jnp.where(last, zeros, u[HALO_U + tm:2 * HALO_U + tm])
    else:
        ubuf[0:HALO_U, :] = zeros
        ubuf[tile, :] = u
        ubuf[HALO_U + tm:2 * HALO_U + tm, :] = zeros
    gbuf[:, 0:w_sc] = proj(w_sc) * _silu(proj(3 * w_sc))
    gbuf[:, w_sc:2 * w_sc] = _silu(proj(o_cf + 2 * w_sc))

    def store(ref, col, value):
        ref[0, :, col:col + MXU_N] = value.astype(BF16)

    def store_value_slabs(col, value):
        ones = jnp.ones((tm, LANES), BF16)
        for k in range(MXU_N // LANES):
            slab = w_na + (col // LANES + k) * MXU_N
            zkv_ref[0, :, slab:slab + LANES] = value[:, k * LANES:(k + 1) * LANES].astype(BF16)
            zkv_ref[0, :, slab + LANES:slab + MXU_N] = ones

    q_scale = HEAD_DIM ** -0.5 * LOG2_E
    attn_units = []
    for c in range(0, w_na, MXU_N):
        attn_units += [
            lambda c=c: store(zq_ref, c, proj(o_na + c) * q_scale),
            lambda c=c: store(zkv_ref, c, proj(o_na + w_na + c)),
            lambda c=c: store_value_slabs(c, proj(o_na + 2 * w_na + c)),
            lambda c=c: store(zq_ref, w_na + c, _silu(proj(o_na + 3 * w_na + c))),
        ]
    conv_units = _conv_units(ubuf, gbuf, zcbuf, csc_ref, ccf_ref, cfb_ref, lng_ref, lnb_ref, tm, w_sc)
    per = max(len(conv_units) // len(attn_units), 1)
    chunk = tm // len(conv_units)
    group = OUT_GROUP // chunk
    for k, conv_unit in enumerate(conv_units):
        if k % per == 0 and attn_units:
            attn_units.pop(0)()
        conv_unit()
        if (k + 1) % group == 0:
            rows = slice((k + 1 - group) * chunk, (k + 1) * chunk)
            yc_ref[0, rows, :] = jnp.dot(zcbuf[rows, :], woutc_ref[...], preferred_element_type=F32)
    for attn_unit in attn_units:
        attn_unit()


def _proj(x, g, scale, shift, w, csc, ccf, cfb, lng, lnb, woutc, layer, *, tm):
    b, n, d = x.shape
    w_sc = d // 4
    nt = n // tm
    has_halo = nt > 1
    kv_width = d // 2 + (d // 2 // LANES) * MXU_N
    sb = 0 if scale.shape[0] == 1 else 1
    per = tm // HALO_U

    def const(shape):
        return pl.BlockSpec(shape, lambda bi, i: (0,) * len(shape), pipeline_mode=pl.Buffered(1))

    vec = pl.BlockSpec((1, 1, d), lambda bi, i: (bi * sb, 0, 0))
    out = lambda width: pl.BlockSpec((1, tm, width), lambda bi, i: (bi, i, 0))
    x_specs = [pl.BlockSpec((1, tm, d), lambda bi, i: (bi, i, 0))]
    x_args = [x]
    if has_halo:
        x_specs += [
            pl.BlockSpec((1, HALO_U, d), lambda bi, i: (bi, jnp.maximum(i * per - 1, 0), 0)),
            pl.BlockSpec((1, HALO_U, d), lambda bi, i: (bi, jnp.minimum((i + 1) * per, n // HALO_U - 1), 0)),
        ]
        x_args += [x, x]
    return pl.pallas_call(
        functools.partial(_proj_kernel, d=d, tm=tm, has_halo=has_halo),
        grid=(b, nt),
        in_specs=x_specs + [
            const((1, d)), vec, vec, _layer_block(w, layer),
            const(csc.shape), const(ccf.shape), const(cfb.shape), const(lng.shape), const(lnb.shape),
            _layer_block(woutc, layer),
        ],
        out_specs=[out(d), out(d), out(kv_width)],
        out_shape=[
            jax.ShapeDtypeStruct((b, n, d), BF16),
            jax.ShapeDtypeStruct((b, n, d), F32),
            jax.ShapeDtypeStruct((b, n, kv_width), BF16),
        ],
        scratch_shapes=[
            pltpu.VMEM((tm + 2 * HALO_U, d), BF16),
            pltpu.VMEM((tm + 2 * HALO_U, 2 * w_sc), F32),
            pltpu.VMEM((tm, 2 * w_sc), F32),
            pltpu.VMEM((tm, 2 * w_sc), BF16),
        ],
        compiler_params=pltpu.CompilerParams(
            dimension_semantics=("parallel", "parallel"), vmem_limit_bytes=VMEM_LIMIT),
        name="proj_conv",
    )(*x_args, g, scale, shift, w, csc, ccf, cfb, lng, lnb, woutc)


def _kv_proj_kernel(x_ref, g_ref, scale_ref, shift_ref, w_ref, zkv_ref, *, d):
    w_na = d // 2
    x = x_ref[0]
    ms = jnp.mean(x * x, axis=-1, keepdims=True)
    h = (x * lax.rsqrt(ms + NORM_EPS) * (g_ref[...] * (1.0 + scale_ref[0])) + shift_ref[0]).astype(BF16)
    ones = jnp.ones((x.shape[0], LANES), BF16)
    for c in range(0, w_na, MXU_N):
        k = jnp.dot(h, w_ref[:, d + w_na + c:d + w_na + c + MXU_N], preferred_element_type=F32)
        v = jnp.dot(h, w_ref[:, d + 2 * w_na + c:d + 2 * w_na + c + MXU_N], preferred_element_type=F32)
        zkv_ref[0, :, c:c + MXU_N] = k.astype(BF16)
        for t in range(MXU_N // LANES):
            slab = w_na + (c // LANES + t) * MXU_N
            zkv_ref[0, :, slab:slab + LANES] = v[:, t * LANES:(t + 1) * LANES].astype(BF16)
            zkv_ref[0, :, slab + LANES:slab + MXU_N] = ones


def _kv_proj(x, g, scale, shift, w, layer):
    b, n, d = x.shape
    kv_width = d // 2 + (d // 2 // LANES) * MXU_N
    vec = pl.BlockSpec((1, 1, d), lambda bi: (0, 0, 0))
    return pl.pallas_call(
        functools.partial(_kv_proj_kernel, d=d),
        grid=(b,),
        in_specs=[
            pl.BlockSpec((1, n, d), lambda bi: (bi, 0, 0)),
            pl.BlockSpec((1, d), lambda bi: (0, 0)),
            vec, vec,
            _layer_block(w, layer),
        ],
        out_specs=pl.BlockSpec((1, n, kv_width), lambda bi: (bi, 0, 0)),
        out_shape=jax.ShapeDtypeStruct((b, n, kv_width), BF16),
        compiler_params=pltpu.CompilerParams(
            dimension_semantics=("parallel",), vmem_limit_bytes=VMEM_LIMIT),
        name="ctx_kv_proj",
    )(x, g, scale, shift, w)


def _split_heads(q2):
    lane = lax.broadcasted_iota(jnp.int32, q2.shape, 1)
    zero = jnp.zeros_like(q2)
    return jnp.concatenate([jnp.where(lane < HEAD_DIM, q2, zero),
                            jnp.where(lane >= HEAD_DIM, q2, zero)], axis=0)


def _merge_heads(o2, rows):
    lane = lax.broadcasted_iota(jnp.int32, (rows, LANES), 1)
    return jnp.where(lane < HEAD_DIM, o2[0:rows], o2[rows:2 * rows])


def _lane_tiles(a):
    return [a[:, i:i + LANES] for i in range(0, a.shape[1], LANES)]


_NT = (((1,), (1,)), ((), ()))


def _residual_out(x_ref, gate_ref, yc_ref, ybuf, wouta_ref):
    y = yc_ref[0] + jnp.dot(ybuf[...], wouta_ref[...], preferred_element_type=F32)
    return x_ref[0] + gate_ref[0] * y


def _latent_mixer_kernel(x_ref, gate_ref, zq_ref, yc_ref, kv_ref, kvc_ref, bias_ref, wouta_ref, fg_ref,
                         o_ref, ybuf, *, tm, d, rows_total, final):
    w_na = d // 2
    n_pairs = w_na // LANES
    tile_rows = tm // GRID_W
    kv_rows = kv_ref.shape[1] // GRID_W
    r0 = pl.program_id(1) * tile_rows
    kv_row0 = jnp.clip(r0 - NA_ROWS // 2, 0, rows_total - kv_rows)
    n_keys = NA_ROWS * GRID_W

    row_params = []
    for j in range(tile_rows):
        r = r0 + j
        rs = jnp.clip(r - NA_ROWS // 2, 0, rows_total - NA_ROWS)
        kstart = pl.multiple_of((rs - kv_row0) * GRID_W, GRID_W)
        row_params.append((kstart, rs - r + (NA_ROWS - 1)))

    def scores(it):
        j, p = divmod(it, n_pairs)
        kstart, bstart = row_params[j]
        lanes = slice(p * LANES, (p + 1) * LANES)
        qs = _split_heads(zq_ref[0, j * GRID_W:(j + 1) * GRID_W, lanes])
        kw = kv_ref[0, pl.ds(kstart, n_keys), lanes]
        kc = kvc_ref[0, :, lanes]
        s_w = lax.dot_general(qs, kw, _NT, preferred_element_type=F32)
        s_c = lax.dot_general(qs, kc, _NT, preferred_element_type=F32)
        bias = jnp.concatenate(
            [jnp.concatenate([bias_ref[2 * p, bstart + 2 * u], bias_ref[2 * p + 1, bstart + 2 * u]], axis=0)
             for u in range(NA_ROWS // 2)], axis=-1)
        s_w = s_w + bias
        m = jnp.max(functools.reduce(jnp.maximum, _lane_tiles(s_w) + _lane_tiles(s_c)), axis=-1, keepdims=True)
        return s_w, s_c, m

    def probs(s_w, s_c, m):
        return jnp.exp2(s_w - m).astype(BF16), jnp.exp2(s_c - m).astype(BF16)

    def outputs(it, p_w, p_c):
        j, p = divmod(it, n_pairs)
        kstart = row_params[j][0]
        slab = slice(w_na + p * MXU_N, w_na + (p + 1) * MXU_N)
        vw = kv_ref[0, pl.ds(kstart, n_keys), slab]
        vc = kvc_ref[0, :, slab]
        oa = jnp.dot(p_w, vw, preferred_element_type=F32) + jnp.dot(p_c, vc, preferred_element_type=F32)
        o = _merge_heads(oa[:, 0:LANES] / oa[:, LANES:MXU_N], GRID_W)
        rows = slice(j * GRID_W, (j + 1) * GRID_W)
        gate = zq_ref[0, rows, w_na + p * LANES:w_na + (p + 1) * LANES].astype(F32)
        ybuf[rows, p * LANES:(p + 1) * LANES] = (o * gate).astype(BF16)

    n_items = tile_rows * n_pairs
    st_scores, st_probs = {}, {}
    for k in range(n_items + OUTPUT_LAG):
        if k < n_items:
            st_scores[k] = scores(k)
        if 0 <= k - PROBS_LAG < n_items:
            st_probs[k - PROBS_LAG] = probs(*st_scores.pop(k - PROBS_LAG))
        if 0 <= k - OUTPUT_LAG < n_items:
            outputs(k - OUTPUT_LAG, *st_probs.pop(k - OUTPUT_LAG))

    xn = _residual_out(x_ref, gate_ref, yc_ref, ybuf, wouta_ref)
    if final:
        ms = jnp.mean(xn * xn, axis=-1, keepdims=True)
        xn = xn * lax.rsqrt(ms + NORM_EPS) * fg_ref[...]
    o_ref[0] = xn


def _ctx_mixer_kernel(x_ref, gate_ref, zq_ref, yc_ref, zkv_ref, wouta_ref,
                      o_ref, ybuf, *, tm, d):
    w_na = d // 2
    for p in range(w_na // LANES):
        lanes = slice(p * LANES, (p + 1) * LANES)
        qs = _split_heads(zq_ref[0, :, lanes])
        kc = zkv_ref[0, :, lanes]
        vc = zkv_ref[0, :, w_na + p * MXU_N:w_na + p * MXU_N + LANES]
        s = lax.dot_general(qs, kc, _NT, preferred_element_type=F32)
        m = jnp.max(s, axis=-1, keepdims=True)
        pr = jnp.exp2(s - m)
        den = jnp.sum(pr, axis=-1, keepdims=True)
        o2 = jnp.dot(pr.astype(BF16), vc, preferred_element_type=F32) / den
        o = _merge_heads(o2, tm)
        gate = zq_ref[0, :, w_na + p * LANES:w_na + (p + 1) * LANES].astype(F32)
        ybuf[:, lanes] = (o * gate).astype(BF16)
    o_ref[0] = _residual_out(x_ref, gate_ref, yc_ref, ybuf, wouta_ref)


def _latent_mixer(x, gate, zq, yc, zkv, zkv_c, bias, wouta, fg, layer, *, tm, final):
    b, n, d = x.shape
    n_ctx, kv_width = zkv_c.shape[1], zkv.shape[2]
    nt = n // tm
    kv_window = tm + NA_ROWS * GRID_W
    halo = (NA_ROWS // 2) * GRID_W

    def const(shape):
        return pl.BlockSpec(shape, lambda bi, i: (0,) * len(shape), pipeline_mode=pl.Buffered(1))

    tile = lambda width: pl.BlockSpec((1, tm, width), lambda bi, i: (bi, i, 0))
    in_specs = [
        tile(d),
        pl.BlockSpec((1, 1, d), lambda bi, i: (bi, 0, 0)),
        tile(d), tile(d),
        pl.BlockSpec((pl.Element(1), pl.Element(kv_window), pl.Element(kv_width)),
                     lambda bi, i: (bi, halo * jnp.clip(i * (tm // halo) - 1, 0, (n - kv_window) // halo), 0)),
        pl.BlockSpec((1, n_ctx, kv_width), lambda bi, i: (bi, 0, 0)),
        _layer_block(bias, layer), _layer_block(wouta, layer), const(fg.shape),
    ]
    return pl.pallas_call(
        functools.partial(_latent_mixer_kernel, tm=tm, d=d, rows_total=n // GRID_W, final=final),
        grid=(b, nt),
        in_specs=in_specs,
        out_specs=tile(d),
        out_shape=jax.ShapeDtypeStruct((b, n, d), F32),
        scratch_shapes=[pltpu.VMEM((tm, d // 2), BF16)],
        compiler_params=pltpu.CompilerParams(
            dimension_semantics=("parallel", "parallel"), vmem_limit_bytes=VMEM_LIMIT),
        name="latent_mixer",
    )(x, gate, zq, yc, zkv, zkv_c, bias, wouta, fg)


def _ctx_mixer(x, gate, zq, yc, zkv, wouta, layer):
    b, n, d = x.shape
    tile = lambda width: pl.BlockSpec((1, n, width), lambda bi: (bi, 0, 0))
    return pl.pallas_call(
        functools.partial(_ctx_mixer_kernel, tm=n, d=d),
        grid=(b,),
        in_specs=[
            tile(d),
            pl.BlockSpec((1, 1, d), lambda bi: (0, 0, 0)),
            tile(d), tile(d), tile(zkv.shape[2]),
            _layer_block(wouta, layer),
        ],
        out_specs=tile(d),
        out_shape=jax.ShapeDtypeStruct((b, n, d), F32),
        scratch_shapes=[pltpu.VMEM((n, d // 2), BF16)],
        compiler_params=pltpu.CompilerParams(
            dimension_semantics=("parallel",), vmem_limit_bytes=VMEM_LIMIT),
        name="ctx_mixer",
    )(x, gate, zq, yc, zkv, wouta)


def _bias_tables(rpb):
    cols = np.arange(GRID_W)
    col_start = np.clip(cols - NA_COLS // 2, 0, GRID_W - NA_COLS)
    inside = (cols[None, :] >= col_start[:, None]) & (cols[None, :] < col_start[:, None] + NA_COLS)
    rel = cols[None, :] - cols[:, None] + (NA_COLS - 1)
    select = (np.arange(2 * NA_COLS - 1)[:, None, None] == rel[None]) & inside[None]
    tab = jnp.einsum('lhdj,jqk->lhdqk', rpb * LOG2_E, jnp.asarray(select, F32),
                     precision=lax.Precision.HIGHEST)
    tab = tab + jnp.asarray(np.where(inside, 0.0, MASK_BIAS), F32)
    nxt = jnp.concatenate([tab[:, :, 1:], jnp.full_like(tab[:, :, :1], MASK_BIAS)], axis=2)
    return jnp.concatenate([tab, nxt], axis=-1).astype(F32)


def kernel(x, c, ctx, c_ctx, norm_g, w_ada, b_ada, w_in, conv_sc, rpb, conv_cf, conv_cf_b,
           ln_cf_g, ln_cf_b, w_out, final_g):
    depth = w_in.shape[0]
    b, n, d = x.shape
    n_ctx = ctx.shape[1]
    w_sc = d // 4
    mod_rows = 8 * ((b + 1 + 7) // 8)
    cc = jnp.concatenate([c, c_ctx[None, :], jnp.zeros((mod_rows - b - 1, d), F32)], axis=0)
    mods = _modulation(cc, w_ada, b_ada)
    w_in_b = w_in.astype(BF16)
    w_out_b = w_out.astype(BF16)
    w_out_c = jnp.concatenate([w_out_b[:, 0:w_sc], w_out_b[:, 3 * w_sc:4 * w_sc]], axis=1)
    w_out_a = w_out_b[:, w_sc:3 * w_sc]
    bias = _bias_tables(rpb)
    fg = final_g.reshape(1, d)

    for l in range(depth):
        final = l == depth - 1
        shift = mods[l, :b, 0:d].reshape(b, 1, d)
        scale = mods[l, :b, d:2 * d].reshape(b, 1, d)
        gate = mods[l, :b, 2 * d:3 * d].reshape(b, 1, d)
        shift_c = mods[l, b:b + 1, 0:d].reshape(1, 1, d)
        scale_c = mods[l, b:b + 1, d:2 * d].reshape(1, 1, d)
        gate_c = mods[l, b:b + 1, 2 * d:3 * d].reshape(1, 1, d)
        g = norm_g[l].reshape(1, d)
        conv = (conv_sc[l], conv_cf[l]) + tuple(v[l].reshape(1, -1) for v in (conv_cf_b, ln_cf_g, ln_cf_b))

        if final:
            zkv_c = _kv_proj(ctx, g, scale_c, shift_c, w_in_b, l)
        else:
            zq_c, yc_c, zkv_c = _proj(ctx, g, scale_c, shift_c, w_in_b, *conv, w_out_c, l, tm=n_ctx)
        zq, yc, zkv = _proj(x, g, scale, shift, w_in_b, *conv, w_out_c, l, tm=2 * TM_LATENT)
        x = _latent_mixer(x, gate, zq, yc, zkv, zkv_c, bias, w_out_a, fg, l, tm=2 * TM_LATENT, final=final)
        if not final:
            ctx = _ctx_mixer(ctx, gate_c, zq_c, yc_c, zkv_c, w_out_a, l)
    return x
```

```python
import functools

import numpy as np
import jax
import jax.numpy as jnp
from jax import lax
from jax.experimental import pallas as pl
from jax.experimental.pallas import tpu as pltpu

F32 = jnp.float32
BF16 = jnp.bfloat16

GRID_W = 64
HEAD_DIM = 64
NA_ROWS = 8
NA_COLS = 16
SC_K = 3
CF_K = 31
NORM_EPS = 1e-6
LN_EPS = 1e-5
MASK_BIAS = -1e30

LANES = 128
SUBLANES = 8
MXU_N = 256
HALO_U = 16
LOG2_E = 1.4426950408889634
VMEM_LIMIT = 56 * 1024 * 1024
TM_LATENT = 1024
PROBS_LAG = 2
OUTPUT_LAG = 5
OUT_GROUP = 256


def _silu(v):
    return v * (1.0 / (1.0 + jnp.exp(-v)))


def _layer_block(stacked, layer):
    zeros = (0,) * (stacked.ndim - 1)
    return pl.BlockSpec((None,) + stacked.shape[1:], lambda *_: (layer,) + zeros, pipeline_mode=pl.Buffered(1))


def _sigmoid(v):
    return 1.0 / (1.0 + jnp.exp(-v))


def _mod_kernel(c_ref, w_ref, b_ref, o_ref):
    a = _silu(c_ref[...]).astype(BF16)
    o_ref[0] = jnp.dot(a, w_ref[0].astype(BF16), preferred_element_type=F32) + b_ref[0]


def _modulation(cc, w_ada, b_ada):
    depth, d, d3 = w_ada.shape
    rows = cc.shape[0]
    tn = 1024
    return pl.pallas_call(
        _mod_kernel,
        grid=(depth, d3 // tn),
        in_specs=[
            pl.BlockSpec((rows, d), lambda l, n: (0, 0)),
            pl.BlockSpec((1, d, tn), lambda l, n: (l, 0, n)),
            pl.BlockSpec((1, 1, tn), lambda l, n: (l, 0, n)),
        ],
        out_specs=pl.BlockSpec((1, rows, tn), lambda l, n: (l, 0, n)),
        out_shape=jax.ShapeDtypeStruct((depth, rows, d3), F32),
        compiler_params=pltpu.CompilerParams(vmem_limit_bytes=VMEM_LIMIT),
        name="adaln_mod",
    )(cc, w_ada, b_ada.reshape(depth, 1, d3))


class _DepthwiseConv:
    def __init__(self, ubuf, w_ref, lane0, w_lane0, width, ksize):
        self.ubuf, self.w_ref, self.lanes, self.ksize = ubuf, w_ref, slice(lane0, lane0 + width), ksize
        self.w_lanes = slice(w_lane0, w_lane0 + width)
        self.shifts = sorted({(j - ksize // 2) % SUBLANES for j in range(ksize)})
        self.rotated = {}
        self.row = lax.broadcasted_iota(jnp.int32, (SUBLANES, width), 0)

    def _rotated_partial(self, shift, row_tile):
        key = (shift, row_tile)
        if key not in self.rotated:
            acc = None
            for j in range(self.ksize):
                offset = j - self.ksize // 2
                if offset % SUBLANES != shift:
                    continue
                start = HALO_U + row_tile * SUBLANES + offset - shift
                term = self.ubuf[start:start + SUBLANES, self.lanes] * self.w_ref[j:j + 1, self.w_lanes]
                acc = term if acc is None else acc + term
            self.rotated[key] = acc if shift == 0 else pltpu.roll(acc, SUBLANES - shift, 0)
        return self.rotated[key]

    def tile(self, row_tile):
        out = None
        for shift in self.shifts:
            part = self._rotated_partial(shift, row_tile)
            if shift:
                part = jnp.where(self.row < SUBLANES - shift, part, self._rotated_partial(shift, row_tile + 1))
            out = part if out is None else out + part
        self.rotated = {k: v for k, v in self.rotated.items() if k[1] > row_tile}
        return out

    def rows(self, base, rows):
        return jnp.concatenate([self.tile(base // SUBLANES + k) for k in range(rows // SUBLANES)], axis=0)


def _conv_units(ubuf, gbuf, zcbuf, csc_ref, ccf_ref, cfb_ref, lng_ref, lnb_ref, tm, w_sc):
    chunk = 2 * SUBLANES
    conv_sc = [_DepthwiseConv(ubuf, csc_ref, l, l, LANES, SC_K) for l in range(0, w_sc, LANES)]
    conv_cf = [_DepthwiseConv(ubuf, ccf_ref, w_sc + l, l, LANES, CF_K) for l in range(0, w_sc, LANES)]

    def unit(c0):
        rows = slice(c0, c0 + chunk)
        ya = jnp.concatenate([cv.rows(c0, chunk) for cv in conv_sc], axis=-1) * gbuf[rows, 0:w_sc]
        zcbuf[rows, 0:w_sc] = ya.astype(BF16)
        u = jnp.concatenate([cv.rows(c0, chunk) for cv in conv_cf], axis=-1) + cfb_ref[...]
        mu = jnp.mean(u, axis=-1, keepdims=True)
        uc = u - mu
        var = jnp.mean(uc * uc, axis=-1, keepdims=True)
        un = uc * lax.rsqrt(var + LN_EPS) * lng_ref[...] + lnb_ref[...]
        yc = _silu(un) * gbuf[rows, w_sc:2 * w_sc]
        zcbuf[rows, w_sc:2 * w_sc] = yc.astype(BF16)

    return [functools.partial(unit, c0) for c0 in range(0, tm, chunk)]


def _proj_kernel(*refs, d, tm, has_halo):
    if has_halo:
        x_ref, xp_ref, xn_ref = refs[:3]
        refs = refs[3:]
    else:
        x_ref, refs = refs[0], refs[1:]
    (g_ref, scale_ref, shift_ref, w_ref, csc_ref, ccf_ref, cfb_ref, lng_ref, lnb_ref, woutc_ref,
     zq_ref, yc_ref, zkv_ref, hbuf, ubuf, gbuf, zcbuf) = refs
    w_sc = d // 4
    w_na = d // 2
    o_na = 4 * w_sc
    o_cf = o_na + 4 * w_na
    mul = g_ref[...] * (1.0 + scale_ref[0])

    def norm(x):
        ms = jnp.mean(x * x, axis=-1, keepdims=True)
        return (x * lax.rsqrt(ms + NORM_EPS) * mul + shift_ref[0]).astype(BF16)

    half = tm // 2
    for r in (0, half):
        hbuf[HALO_U + r:HALO_U + r + half, :] = norm(x_ref[0, r:r + half, :])
    tile = slice(HALO_U, HALO_U + tm)
    if has_halo:
        hbuf[0:HALO_U, :] = norm(xp_ref[0])
        hbuf[HALO_U + tm:2 * HALO_U + tm, :] = norm(xn_ref[0])
        wide = slice(0, tm + 2 * HALO_U)
        first = pl.program_id(1) == 0
        last = pl.program_id(1) == pl.num_programs(1) - 1
    else:
        wide = tile

    def proj(start, rows=tile):
        return jnp.dot(hbuf[rows, :], w_ref[:, start:start + MXU_N], preferred_element_type=F32)

    u = jnp.concatenate([proj(2 * w_sc, wide) * proj(0, wide),
                         proj(o_cf, wide) * _sigmoid(proj(o_cf + w_sc, wide))], axis=-1)
    zeros = jnp.zeros((HALO_U, 2 * w_sc), F32)
    if has_halo:
        ubuf[0:HALO_U, :] = jnp.where(first, zeros, u[0:HALO_U])
        ubuf[tile, :] = u[HALO_U:HALO_U + tm]
        ubuf[HALO_U + tm:2 * HALO_U + tm, :] = jnp.where(last, zeros, u[HALO_U + tm:2 * HALO_U + tm])
    else:
        ubuf[0:HALO_U, :] = zeros
        ubuf[tile, :] = u
        ubuf[HALO_U + tm:2 * HALO_U + tm, :] = zeros
    gbuf[:, 0:w_sc] = proj(w_sc) * _silu(proj(3 * w_sc))
    gbuf[:, w_sc:2 * w_sc] = _silu(proj(o_cf + 2 * w_sc))

    def store(ref, col, value):
        ref[0, :, col:col + MXU_N] = value.astype(BF16)

    def store_value_slabs(col, value):
        ones = jnp.ones((tm, LANES), BF16)
        for k in range(MXU_N // LANES):
            slab = w_na + (col // LANES + k) * MXU_N
            zkv_ref[0, :, slab:slab + LANES] = value[:, k * LANES:(k + 1) * LANES].astype(BF16)
            zkv_ref[0, :, slab + LANES:slab + MXU_N] = ones

    q_scale = HEAD_DIM ** -0.5 * LOG2_E
    attn_units = []
    for c in range(0, w_na, MXU_N):
        attn_units += [
            lambda c=c: store(zq_ref, c, proj(o_na + c) * q_scale),
            lambda c=c: store(zkv_ref, c, proj(o_na + w_na + c)),
            lambda c=c: store_value_slabs(c, proj(o_na + 2 * w_na + c)),
            lambda c=c: store(zq_ref, w_na + c, _silu(proj(o_na + 3 * w_na + c))),
        ]
    conv_units = _conv_units(ubuf, gbuf, zcbuf, csc_ref, ccf_ref, cfb_ref, lng_ref, lnb_ref, tm, w_sc)
    per = max(len(conv_units) // len(attn_units), 1)
    chunk = tm // len(conv_units)
    group = OUT_GROUP // chunk
    for k, conv_unit in enumerate(conv_units):
        if k % per == 0 and attn_units:
            attn_units.pop(0)()
        conv_unit()
        if (k + 1) % group == 0:
            rows = slice((k + 1 - group) * chunk, (k + 1) * chunk)
            yc_ref[0, rows, :] = jnp.dot(zcbuf[rows, :], woutc_ref[...], preferred_element_type=F32)
    for attn_unit in attn_units:
        attn_unit()


def _proj(x, g, scale, shift, w, csc, ccf, cfb, lng, lnb, woutc, layer, *, tm):
    b, n, d = x.shape
    w_sc = d // 4
    nt = n // tm
    has_halo = nt > 1
    kv_width = d // 2 + (d // 2 // LANES) * MXU_N
    sb = 0 if scale.shape[0] == 1 else 1
    per = tm // HALO_U

    def const(shape):
        return pl.BlockSpec(shape, lambda bi, i: (0,) * len(shape), pipeline_mode=pl.Buffered(1))

    vec = pl.BlockSpec((1, 1, d), lambda bi, i: (bi * sb, 0, 0))
    out = lambda width: pl.BlockSpec((1, tm, width), lambda bi, i: (bi, i, 0))
    x_specs = [pl.BlockSpec((1, tm, d), lambda bi, i: (bi, i, 0))]
    x_args = [x]
    if has_halo:
        x_specs += [
            pl.BlockSpec((1, HALO_U, d), lambda bi, i: (bi, jnp.maximum(i * per - 1, 0), 0)),
            pl.BlockSpec((1, HALO_U, d), lambda bi, i: (bi, jnp.minimum((i + 1) * per, n // HALO_U - 1), 0)),
        ]
        x_args += [x, x]
    return pl.pallas_call(
        functools.partial(_proj_kernel, d=d, tm=tm, has_halo=has_halo),
        grid=(b, nt),
        in_specs=x_specs + [
            const((1, d)), vec, vec, _layer_block(w, layer),
            const(csc.shape), const(ccf.shape), const(cfb.shape), const(lng.shape), const(lnb.shape),
            _layer_block(woutc, layer),
        ],
        out_specs=[out(d), out(d), out(kv_width)],
        out_shape=[
            jax.ShapeDtypeStruct((b, n, d), BF16),
            jax.ShapeDtypeStruct((b, n, d), F32),
            jax.ShapeDtypeStruct((b, n, kv_width), BF16),
        ],
        scratch_shapes=[
            pltpu.VMEM((tm + 2 * HALO_U, d), BF16),
            pltpu.VMEM((tm + 2 * HALO_U, 2 * w_sc), F32),
            pltpu.VMEM((tm, 2 * w_sc), F32),
            pltpu.VMEM((tm, 2 * w_sc), BF16),
        ],
        compiler_params=pltpu.CompilerParams(
            dimension_semantics=("parallel", "parallel"), vmem_limit_bytes=VMEM_LIMIT),
        name="proj_conv",
    )(*x_args, g, scale, shift, w, csc, ccf, cfb, lng, lnb, woutc)


def _kv_proj_kernel(x_ref, g_ref, scale_ref, shift_ref, w_ref, zkv_ref, *, d):
    w_na = d // 2
    x = x_ref[0]
    ms = jnp.mean(x * x, axis=-1, keepdims=True)
    h = (x * lax.rsqrt(ms + NORM_EPS) * (g_ref[...] * (1.0 + scale_ref[0])) + shift_ref[0]).astype(BF16)
    ones = jnp.ones((x.shape[0], LANES), BF16)
    for c in range(0, w_na, MXU_N):
        k = jnp.dot(h, w_ref[:, d + w_na + c:d + w_na + c + MXU_N], preferred_element_type=F32)
        v = jnp.dot(h, w_ref[:, d + 2 * w_na + c:d + 2 * w_na + c + MXU_N], preferred_element_type=F32)
        zkv_ref[0, :, c:c + MXU_N] = k.astype(BF16)
        for t in range(MXU_N // LANES):
            slab = w_na + (c // LANES + t) * MXU_N
            zkv_ref[0, :, slab:slab + LANES] = v[:, t * LANES:(t + 1) * LANES].astype(BF16)
            zkv_ref[0, :, slab + LANES:slab + MXU_N] = ones


def _kv_proj(x, g, scale, shift, w, layer):
    b, n, d = x.shape
    kv_width = d // 2 + (d // 2 // LANES) * MXU_N
    vec = pl.BlockSpec((1, 1, d), lambda bi: (0, 0, 0))
    return pl.pallas_call(
        functools.partial(_kv_proj_kernel, d=d),
        grid=(b,),
        in_specs=[
            pl.BlockSpec((1, n, d), lambda bi: (bi, 0, 0)),
            pl.BlockSpec((1, d), lambda bi: (0, 0)),
            vec, vec,
            _layer_block(w, layer),
        ],
        out_specs=pl.BlockSpec((1, n, kv_width), lambda bi: (bi, 0, 0)),
        out_shape=jax.ShapeDtypeStruct((b, n, kv_width), BF16),
        compiler_params=pltpu.CompilerParams(
            dimension_semantics=("parallel",), vmem_limit_bytes=VMEM_LIMIT),
        name="ctx_kv_proj",
    )(x, g, scale, shift, w)


def _split_heads(q2):
    lane = lax.broadcasted_iota(jnp.int32, q2.shape, 1)
    zero = jnp.zeros_like(q2)
    return jnp.concatenate([jnp.where(lane < HEAD_DIM, q2, zero),
                            jnp.where(lane >= HEAD_DIM, q2, zero)], axis=0)


def _merge_heads(o2, rows):
    lane = lax.broadcasted_iota(jnp.int32, (rows, LANES), 1)
    return jnp.where(lane < HEAD_DIM, o2[0:rows], o2[rows:2 * rows])


def _lane_tiles(a):
    return [a[:, i:i + LANES] for i in range(0, a.shape[1], LANES)]


_NT = (((1,), (1,)), ((), ()))


def _residual_out(x_ref, gate_ref, yc_ref, ybuf, wouta_ref):
    y = yc_ref[0] + jnp.dot(ybuf[...], wouta_ref[...], preferred_element_type=F32)
    return x_ref[0] + gate_ref[0] * y


def _latent_mixer_kernel(x_ref, gate_ref, zq_ref, yc_ref, kv_ref, kvc_ref, bias_ref, wouta_ref, fg_ref,
                         o_ref, ybuf, *, tm, d, rows_total, final):
    w_na = d // 2
    n_pairs = w_na // LANES
    tile_rows = tm // GRID_W
    kv_rows = kv_ref.shape[1] // GRID_W
    r0 = pl.program_id(1) * tile_rows
    kv_row0 = jnp.clip(r0 - NA_ROWS // 2, 0, rows_total - kv_rows)
    n_keys = NA_ROWS * GRID_W

    row_params = []
    for j in range(tile_rows):
        r = r0 + j
        rs = jnp.clip(r - NA_ROWS // 2, 0, rows_total - NA_ROWS)
        kstart = pl.multiple_of((rs - kv_row0) * GRID_W, GRID_W)
        row_params.append((kstart, rs - r + (NA_ROWS - 1)))

    def scores(it):
        j, p = divmod(it, n_pairs)
        kstart, bstart = row_params[j]
        lanes = slice(p * LANES, (p + 1) * LANES)
        qs = _split_heads(zq_ref[0, j * GRID_W:(j + 1) * GRID_W, lanes])
        kw = kv_ref[0, pl.ds(kstart, n_keys), lanes]
        kc = kvc_ref[0, :, lanes]
        s_w = lax.dot_general(qs, kw, _NT, preferred_element_type=F32)
        s_c = lax.dot_general(qs, kc, _NT, preferred_element_type=F32)
        bias = jnp.concatenate(
            [jnp.concatenate([bias_ref[2 * p, bstart + 2 * u], bias_ref[2 * p + 1, bstart + 2 * u]], axis=0)
             for u in range(NA_ROWS // 2)], axis=-1)
        s_w = s_w + bias
        m = jnp.max(functools.reduce(jnp.maximum, _lane_tiles(s_w) + _lane_tiles(s_c)), axis=-1, keepdims=True)
        return s_w, s_c, m

    def probs(s_w, s_c, m):
        return jnp.exp2(s_w - m).astype(BF16), jnp.exp2(s_c - m).astype(BF16)

    def outputs(it, p_w, p_c):
        j, p = divmod(it, n_pairs)
        kstart = row_params[j][0]
        slab = slice(w_na + p * MXU_N, w_na + (p + 1) * MXU_N)
        vw = kv_ref[0, pl.ds(kstart, n_keys), slab]
        vc = kvc_ref[0, :, slab]
        oa = jnp.dot(p_w, vw, preferred_element_type=F32) + jnp.dot(p_c, vc, preferred_element_type=F32)
        o = _merge_heads(oa[:, 0:LANES] / oa[:, LANES:MXU_N], GRID_W)
        rows = slice(j * GRID_W, (j + 1) * GRID_W)
        gate = zq_ref[0, rows, w_na + p * LANES:w_na + (p + 1) * LANES].astype(F32)
        ybuf[rows, p * LANES:(p + 1) * LANES] = (o * gate).astype(BF16)

    n_items = tile_rows * n_pairs
    st_scores, st_probs = {}, {}
    for k in range(n_items + OUTPUT_LAG):
        if k < n_items:
            st_scores[k] = scores(k)
        if 0 <= k - PROBS_LAG < n_items:
            st_probs[k - PROBS_LAG] = probs(*st_scores.pop(k - PROBS_LAG))
        if 0 <= k - OUTPUT_LAG < n_items:
            outputs(k - OUTPUT_LAG, *st_probs.pop(k - OUTPUT_LAG))

    xn = _residual_out(x_ref, gate_ref, yc_ref, ybuf, wouta_ref)
    if final:
        ms = jnp.mean(xn * xn, axis=-1, keepdims=True)
        xn = xn * lax.rsqrt(ms + NORM_EPS) * fg_ref[...]
    o_ref[0] = xn


def _ctx_mixer_kernel(x_ref, gate_ref, zq_ref, yc_ref, zkv_ref, wouta_ref,
                      o_ref, ybuf, *, tm, d):
    w_na = d // 2
    for p in range(w_na // LANES):
        lanes = slice(p * LANES, (p + 1) * LANES)
        qs = _split_heads(zq_ref[0, :, lanes])
        kc = zkv_ref[0, :, lanes]
        vc = zkv_ref[0, :, w_na + p * MXU_N:w_na + p * MXU_N + LANES]
        s = lax.dot_general(qs, kc, _NT, preferred_element_type=F32)
        m = jnp.max(s, axis=-1, keepdims=True)
        pr = jnp.exp2(s - m)
        den = jnp.sum(pr, axis=-1, keepdims=True)
        o2 = jnp.dot(pr.astype(BF16), vc, preferred_element_type=F32) / den
        o = _merge_heads(o2, tm)
        gate = zq_ref[0, :, w_na + p * LANES:w_na + (p + 1) * LANES].astype(F32)
        ybuf[:, lanes] = (o * gate).astype(BF16)
    o_ref[0] = _residual_out(x_ref, gate_ref, yc_ref, ybuf, wouta_ref)


def _latent_mixer(x, gate, zq, yc, zkv, zkv_c, bias, wouta, fg, layer, *, tm, final):
    b, n, d = x.shape
    n_ctx, kv_width = zkv_c.shape[1], zkv.shape[2]
    nt = n // tm
    kv_window = tm + NA_ROWS * GRID_W
    halo = (NA_ROWS // 2) * GRID_W

    def const(shape):
        return pl.BlockSpec(shape, lambda bi, i: (0,) * len(shape), pipeline_mode=pl.Buffered(1))

    tile = lambda width: pl.BlockSpec((1, tm, width), lambda bi, i: (bi, i, 0))
    in_specs = [
        tile(d),
        pl.BlockSpec((1, 1, d), lambda bi, i: (bi, 0, 0)),
        tile(d), tile(d),
        pl.BlockSpec((pl.Element(1), pl.Element(kv_window), pl.Element(kv_width)),
                     lambda bi, i: (bi, halo * jnp.clip(i * (tm // halo) - 1, 0, (n - kv_window) // halo), 0)),
        pl.BlockSpec((1, n_ctx, kv_width), lambda bi, i: (bi, 0, 0)),
        _layer_block(bias, layer), _layer_block(wouta, layer), const(fg.shape),
    ]
    return pl.pallas_call(
        functools.partial(_latent_mixer_kernel, tm=tm, d=d, rows_total=n // GRID_W, final=final),
        grid=(b, nt),
        in_specs=in_specs,
        out_specs=tile(d),
        out_shape=jax.ShapeDtypeStruct((b, n, d), F32),
        scratch_shapes=[pltpu.VMEM((tm, d // 2), BF16)],
        compiler_params=pltpu.CompilerParams(
            dimension_semantics=("parallel", "parallel"), vmem_limit_bytes=VMEM_LIMIT),
        name="latent_mixer",
    )(x, gate, zq, yc, zkv, zkv_c, bias, wouta, fg)


def _ctx_mixer(x, gate, zq, yc, zkv, wouta, layer):
    b, n, d = x.shape
    tile = lambda width: pl.BlockSpec((1, n, width), lambda bi: (bi, 0, 0))
    return pl.pallas_call(
        functools.partial(_ctx_mixer_kernel, tm=n, d=d),
        grid=(b,),
        in_specs=[
            tile(d),
            pl.BlockSpec((1, 1, d), lambda bi: (0, 0, 0)),
            tile(d), tile(d), tile(zkv.shape[2]),
            _layer_block(wouta, layer),
        ],
        out_specs=tile(d),
        out_shape=jax.ShapeDtypeStruct((b, n, d), F32),
        scratch_shapes=[pltpu.VMEM((n, d // 2), BF16)],
        compiler_params=pltpu.CompilerParams(
            dimension_semantics=("parallel",), vmem_limit_bytes=VMEM_LIMIT),
        name="ctx_mixer",
    )(x, gate, zq, yc, zkv, wouta)


def _bias_tables(rpb):
    cols = np.arange(GRID_W)
    col_start = np.clip(cols - NA_COLS // 2, 0, GRID_W - NA_COLS)
    inside = (cols[None, :] >= col_start[:, None]) & (cols[None, :] < col_start[:, None] + NA_COLS)
    rel = cols[None, :] - cols[:, None] + (NA_COLS - 1)
    select = (np.arange(2 * NA_COLS - 1)[:, None, None] == rel[None]) & inside[None]
    tab = jnp.einsum('lhdj,jqk->lhdqk', rpb * LOG2_E, jnp.asarray(select, F32),
                     precision=lax.Precision.HIGHEST)
    tab = tab + jnp.asarray(np.where(inside, 0.0, MASK_BIAS), F32)
    nxt = jnp.concatenate([tab[:, :, 1:], jnp.full_like(tab[:, :, :1], MASK_BIAS)], axis=2)
    return jnp.concatenate([tab, nxt], axis=-1).astype(F32)


def kernel(x, c, ctx, c_ctx, norm_g, w_ada, b_ada, w_in, conv_sc, rpb, conv_cf, conv_cf_b,
           ln_cf_g, ln_cf_b, w_out, final_g):
    depth = w_in.shape[0]
    b, n, d = x.shape
    n_ctx = ctx.shape[1]
    w_sc = d // 4
    mod_rows = 8 * ((b + 1 + 7) // 8)
    cc = jnp.concatenate([c, c_ctx[None, :], jnp.zeros((mod_rows - b - 1, d), F32)], axis=0)
    mods = _modulation(cc, w_ada, b_ada)
    w_in_b = w_in.astype(BF16)
    w_out_b = w_out.astype(BF16)
    w_out_c = jnp.concatenate([w_out_b[:, 0:w_sc], w_out_b[:, 3 * w_sc:4 * w_sc]], axis=1)
    w_out_a = w_out_b[:, w_sc:3 * w_sc]
    bias = _bias_tables(rpb)
    fg = final_g.reshape(1, d)

    for l in range(depth):
        final = l == depth - 1
        shift = mods[l, :b, 0:d].reshape(b, 1, d)
        scale = mods[l, :b, d:2 * d].reshape(b, 1, d)
        gate = mods[l, :b, 2 * d:3 * d].reshape(b, 1, d)
        shift_c = mods[l, b:b + 1, 0:d].reshape(1, 1, d)
        scale_c = mods[l, b:b + 1, d:2 * d].reshape(1, 1, d)
        gate_c = mods[l, b:b + 1, 2 * d:3 * d].reshape(1, 1, d)
        g = norm_g[l].reshape(1, d)
        conv = (conv_sc[l], conv_cf[l]) + tuple(v[l].reshape(1, -1) for v in (conv_cf_b, ln_cf_g, ln_cf_b))

        if final:
            zkv_c = _kv_proj(ctx, g, scale_c, shift_c, w_in_b, l)
        else:
            zq_c, yc_c, zkv_c = _proj(ctx, g, scale_c, shift_c, w_in_b, *conv, w_out_c, l, tm=n_ctx)
        zq, yc, zkv = _proj(x, g, scale, shift, w_in_b, *conv, w_out_c, l, tm=TM_LATENT)
        x = _latent_mixer(x, gate, zq, yc, zkv, zkv_c, bias, w_out_a, fg, l, tm=TM_LATENT, final=final)
        if not final:
            ctx = _ctx_mixer(ctx, gate_c, zq_c, yc_c, zkv_c, w_out_a, l)
    return x
```
